```python
import math
import jax, jax.numpy as jnp
from jax import lax
import numpy as np

D_MODEL = 1024
BATCH = 2
SEQ = 16384
DEPTH = 1

CONV_DIM = D_MODEL
CONV_WIDTH = 3
N_HEADS = 16
HEAD_DIM = 64
ATTN_DIM = N_HEADS * HEAD_DIM
MOBA_BLOCK = 256
TOP_BLOCKS = 3
Q_CHUNK = 32
ROPE_THETA = 10000.0
IN_PROJ_DIM = 3 * CONV_DIM + 3 * ATTN_DIM + 2 * D_MODEL
N_EXPERTS = 32
TOP_K = 4
D_FF = D_MODEL
SWIGLU_LIMIT = 7.0
SWIGLU_ALPHA = 1.702
MOE_BLOCK = 128
EPS = 1e-5

kernel_name = "hybrid_gated_conv_moba_moe_block"


def rmsnorm(x, g):
    xf = x.astype(jnp.float32)
    y = xf * lax.rsqrt(jnp.mean(xf * xf, axis=-1, keepdims=True) + EPS)
    return (y * g.astype(jnp.float32)).astype(x.dtype)


def rope(t, positions):
    half = HEAD_DIM // 2
    inv_freq = ROPE_THETA ** (-jnp.arange(0, half, dtype=jnp.float32) / half)
    ang = positions.astype(jnp.float32)[:, None] * inv_freq[None, :]
    cos = jnp.cos(ang)[None, None]
    sin = jnp.sin(ang)[None, None]
    tf = t.astype(jnp.float32)
    t1, t2 = tf[..., :half], tf[..., half:]
    out = jnp.concatenate([t1 * cos - t2 * sin, t2 * cos + t1 * sin], axis=-1)
    return out.astype(t.dtype)


def short_conv(u, conv_w):
    rhs = conv_w[:, None, :]
    return lax.conv_general_dilated(
        u, rhs, window_strides=(1,), padding=[(CONV_WIDTH - 1, 0)],
        dimension_numbers=("NWC", "WIO", "NWC"), feature_group_count=u.shape[-1])


def moba_attention(q, k, v):
    b, h, s, dh = q.shape
    s_pad = ((s + MOBA_BLOCK - 1) // MOBA_BLOCK) * MOBA_BLOCK
    pad = s_pad - s
    if pad:
        cfg = ((0, 0), (0, 0), (0, pad), (0, 0))
        q, k, v = jnp.pad(q, cfg), jnp.pad(k, cfg), jnp.pad(v, cfg)
    nb = s_pad // MOBA_BLOCK
    n_top = min(TOP_BLOCKS, nb)
    k_blocks = k.reshape(b, h, nb, MOBA_BLOCK, dh)
    v_blocks = v.reshape(b, h, nb, MOBA_BLOCK, dh)
    k_mean = jnp.mean(k_blocks.astype(jnp.float32), axis=3)
    scale = HEAD_DIM ** -0.5
    block_ids = jnp.arange(nb)
    key_offsets = jnp.arange(MOBA_BLOCK)
    gather_blocks = jax.vmap(jax.vmap(lambda kb, idx: kb[idx]))

    def chunk(c):
        start = c * Q_CHUNK
        own = start // MOBA_BLOCK
        q_c = lax.dynamic_slice_in_dim(q, start, Q_CHUNK, axis=2)
        q_pos = start + jnp.arange(Q_CHUNK)
        gate = jnp.einsum("bhqd,bhnd->bhqn", q_c.astype(jnp.float32), k_mean)
        gate = jnp.where((block_ids < own)[None, None, None, :], gate, -jnp.inf)
        g_val, sel = lax.top_k(gate, n_top)
        sel_valid = jnp.isfinite(g_val)
        k_sel = gather_blocks(k_blocks, sel)
        v_sel = gather_blocks(v_blocks, sel)
        s_sel = jnp.einsum("bhqd,bhqtkd->bhqtk", q_c, k_sel).astype(jnp.float32) * scale
        s_sel = jnp.where(sel_valid[..., None], s_sel, -jnp.inf)
        s_sel = s_sel.reshape(b, h, Q_CHUNK, n_top * MOBA_BLOCK)
        k_own = lax.dynamic_slice_in_dim(k, own * MOBA_BLOCK, MOBA_BLOCK, axis=2)
        v_own = lax.dynamic_slice_in_dim(v, own * MOBA_BLOCK, MOBA_BLOCK, axis=2)
        s_own = jnp.einsum("bhqd,bhkd->bhqk", q_c, k_own).astype(jnp.float32) * scale
        causal = (own * MOBA_BLOCK + key_offsets)[None, :] <= q_pos[:, None]
        s_own = jnp.where(causal[None, None], s_own, -jnp.inf)
        p = jax.nn.softmax(jnp.concatenate([s_sel, s_own], axis=-1), axis=-1).astype(v.dtype)
        p_sel = p[..., : n_top * MOBA_BLOCK].reshape(b, h, Q_CHUNK, n_top, MOBA_BLOCK)
        p_own = p[..., n_top * MOBA_BLOCK:]
        return (jnp.einsum("bhqtk,bhqtkd->bhqd", p_sel, v_sel)
                + jnp.einsum("bhqk,bhkd->bhqd", p_own, v_own))

    outs = lax.map(chunk, jnp.arange(s_pad // Q_CHUNK))
    out = outs.transpose(1, 2, 0, 3, 4).reshape(b, h, s_pad, dh)
    return out[:, :, :s]


def moe(hn, router_w, router_b, w_gate_up, b_gate_up, w_down, b_down):
    n, d = hn.shape
    logits = (hn @ router_w + router_b).astype(jnp.float32)
    top_val, top_idx = lax.top_k(logits, TOP_K)
    top_w = jax.nn.softmax(top_val, axis=-1).astype(hn.dtype)
    nk = n * TOP_K
    flat_e = top_idx.reshape(nk)
    flat_tok = jnp.repeat(jnp.arange(n, dtype=jnp.int32), TOP_K)
    flat_w = top_w.reshape(nk)
    order = jnp.argsort(flat_e, stable=True)
    sorted_e = flat_e[order]
    counts = jnp.bincount(flat_e, length=N_EXPERTS)
    padded_counts = ((counts + MOE_BLOCK - 1) // MOE_BLOCK) * MOE_BLOCK
    start_sorted = jnp.cumsum(counts) - counts
    padded_end = jnp.cumsum(padded_counts)
    start_pad = padded_end - padded_counts
    rank = jnp.arange(nk) - start_sorted[sorted_e]
    dest = start_pad[sorted_e] + rank
    n_slots = ((nk + N_EXPERTS * (MOE_BLOCK - 1) + MOE_BLOCK - 1) // MOE_BLOCK) * MOE_BLOCK
    n_blocks = n_slots // MOE_BLOCK
    slot_tok = jnp.full((n_slots,), n, dtype=jnp.int32).at[dest].set(flat_tok[order])
    slot_w = jnp.zeros((n_slots,), hn.dtype).at[dest].set(flat_w[order])
    block_expert = jnp.clip(
        jnp.searchsorted(padded_end, jnp.arange(n_blocks) * MOE_BLOCK, side="right"),
        0, N_EXPERTS - 1)
    h_ext = jnp.concatenate([hn, jnp.zeros((1, d), hn.dtype)], axis=0)
    xs = h_ext[slot_tok].reshape(n_blocks, MOE_BLOCK, d)

    def expert_block(args):
        xb, e = args
        gu = xb @ w_gate_up[e] + b_gate_up[e]
        gate = jnp.minimum(gu[:, :D_FF], SWIGLU_LIMIT)
        up = jnp.clip(gu[:, D_FF:], -SWIGLU_LIMIT, SWIGLU_LIMIT)
        act = (up + 1.0) * (gate * jax.nn.sigmoid(SWIGLU_ALPHA * gate))
        return act @ w_down[e] + b_down[e]

    ys = lax.map(expert_block, (xs, block_expert)).reshape(n_slots, d) * slot_w[:, None]
    out = jnp.zeros((n + 1, d), hn.dtype).at[slot_tok].add(ys)
    return out[:n]


def setup_inputs(seed: int = 0) -> dict:
    key = jax.random.key(seed)
    ks = jax.random.split(key, 16)
    f32 = jnp.float32
    nrm = lambda k, shape, fan_in: jax.random.normal(k, shape, f32) * (fan_in ** -0.5)
    return {
        "x": jax.random.normal(ks[0], (BATCH, SEQ, D_MODEL), f32),
        "norm_mix_g": 1.0 + 0.02 * jax.random.normal(ks[1], (D_MODEL,), f32),
        "w_in": nrm(ks[2], (D_MODEL, IN_PROJ_DIM), D_MODEL),
        "conv_w": nrm(ks[3], (CONV_WIDTH, CONV_DIM), CONV_WIDTH),
        "w_conv_out": nrm(ks[4], (CONV_DIM, D_MODEL), CONV_DIM),
        "w_attn_out": nrm(ks[5], (ATTN_DIM, D_MODEL), ATTN_DIM),
        "w_out": nrm(ks[6], (D_MODEL, D_MODEL), D_MODEL),
        "norm_ffn_g": 1.0 + 0.02 * jax.random.normal(ks[7], (D_MODEL,), f32),
        "router_w": nrm(ks[8], (D_MODEL, N_EXPERTS), D_MODEL),
        "router_b": 0.01 * jax.random.normal(ks[9], (N_EXPERTS,), f32),
        "w_gate_up": nrm(ks[10], (N_EXPERTS, D_MODEL, 2 * D_FF), D_MODEL),
        "b_gate_up": 0.02 * jax.random.normal(ks[11], (N_EXPERTS, 2 * D_FF), f32),
        "w_down": nrm(ks[12], (N_EXPERTS, D_FF, D_MODEL), D_FF),
        "b_down": 0.02 * jax.random.normal(ks[13], (N_EXPERTS, D_MODEL), f32),
        "norm_final_g": 1.0 + 0.02 * jax.random.normal(ks[14], (D_MODEL,), f32),
    }


def reference(x, norm_mix_g, w_in, conv_w, w_conv_out, w_attn_out, w_out, norm_ffn_g,
              router_w, router_b, w_gate_up, b_gate_up, w_down, b_down, norm_final_g):
    b, s, d = x.shape
    positions = jnp.arange(s, dtype=jnp.int32)
    h = x
    for _ in range(DEPTH):
        xn = rmsnorm(h, norm_mix_g)
        z = xn @ w_in
        splits = np.cumsum([CONV_DIM, CONV_DIM, CONV_DIM, ATTN_DIM, ATTN_DIM, ATTN_DIM, D_MODEL]).tolist()
        c_b, c_c, c_x, q, k, v, g_conv, g_attn = jnp.split(z, splits, axis=-1)
        y_conv = (c_b * short_conv(c_c * c_x, conv_w)) @ w_conv_out
        to_heads = lambda t: t.reshape(b, s, N_HEADS, HEAD_DIM).transpose(0, 2, 1, 3)
        qh = rope(to_heads(q), positions)
        kh = rope(to_heads(k), positions)
        vh = to_heads(v)
        attn = moba_attention(qh, kh, vh).transpose(0, 2, 1, 3).reshape(b, s, ATTN_DIM)
        y_attn = attn @ w_attn_out
        merged = jax.nn.sigmoid(g_conv) * y_conv + jax.nn.sigmoid(g_attn) * y_attn
        h = h + merged @ w_out
        hn = rmsnorm(h, norm_ffn_g).reshape(b * s, d)
        h = h + moe(hn, router_w, router_b, w_gate_up, b_gate_up, w_down, b_down).reshape(b, s, d)
    return rmsnorm(h, norm_final_g)
```

```python
import functools

import jax
import jax.numpy as jnp
from jax import lax
from jax.experimental import pallas as pl
from jax.experimental.pallas import tpu as pltpu

D_MODEL = 1024
N_HEADS = 16
HEAD_DIM = 64
HALF_DIM = HEAD_DIM // 2
CONV_WIDTH = 3
MOBA_BLOCK = 256
TOP_BLOCKS = 3
ROPE_THETA = 10000.0
N_EXPERTS = 32
TOP_K = 4
D_FF = D_MODEL
SWIGLU_LIMIT = 7.0
SWIGLU_ALPHA = 1.702
MOE_BLOCK = 128
EPS = 1e-5
IN_PROJ_CHUNKS = 8

LANES = 128
SUBLANES = 8
HEADS_PER_GROUP = LANES // HEAD_DIM
N_HEAD_GROUPS = N_HEADS // HEADS_PER_GROUP
MASKED = -1e30
VMEM_LIMIT = 56 * 1024 * 1024

_MXU_DTYPE = jnp.bfloat16
F32 = jnp.float32


def _dot(a, b):
    return jnp.dot(a, b, preferred_element_type=F32)


def _sigmoid(t):
    return 1.0 / (1.0 + jnp.exp(-t))


def _rope_cols(z, cos, sin_signed):
    lane = lax.broadcasted_iota(jnp.int32, (1, LANES), 1)
    first_half = (lane % HEAD_DIM) < HALF_DIM
    cols = []
    for g in range(D_MODEL // LANES):
        t = z[:, g * LANES:(g + 1) * LANES]
        from_above = pltpu.roll(t, LANES - HALF_DIM, axis=1)
        from_below = pltpu.roll(t, HALF_DIM, axis=1)
        cols.append(t * cos + jnp.where(first_half, from_above, from_below) * sin_signed)
    return jnp.concatenate(cols, axis=1)


def _inproj_kernel(x_ref, g_ref, w_ref, cos_ref, sin_ref,
                   zc_ref, qt_ref, k_ref, vt_ref, gates_ref, kmean_ref, xn_ref, *, tm):
    j = pl.program_id(1)
    kb_per_tile = tm // MOBA_BLOCK

    @pl.when(j == 0)
    def _():
        x = x_ref[...]
        ms = jnp.mean(x * x, axis=-1, keepdims=True)
        xn_ref[...] = ((x * lax.rsqrt(ms + EPS)) * g_ref[...]).astype(xn_ref.dtype)

    z = _dot(xn_ref[...], w_ref[...])

    @pl.when(j < 3)
    def _():
        zc_ref[...] = z

    @pl.when(j == 3)
    def _():
        qt = (_rope_cols(z, cos_ref[...], sin_ref[...]) * (HEAD_DIM ** -0.5)).T
        for c in range(kb_per_tile):
            qt_ref[c] = qt[:, c * MOBA_BLOCK:(c + 1) * MOBA_BLOCK].astype(qt_ref.dtype)

    @pl.when(j == 4)
    def _():
        kr = _rope_cols(z, cos_ref[...], sin_ref[...])
        for c in range(kb_per_tile):
            blk = kr[c * MOBA_BLOCK:(c + 1) * MOBA_BLOCK, :]
            k_ref[c] = blk.astype(k_ref.dtype)
            kmean_ref[0, c:c + 1, :] = jnp.mean(blk, axis=0, keepdims=True)

    @pl.when(j == 5)
    def _():
        vt = z.T
        for c in range(kb_per_tile):
            vt_ref[c] = vt[:, c * MOBA_BLOCK:(c + 1) * MOBA_BLOCK].astype(vt_ref.dtype)

    @pl.when(j >= 6)
    def _():
        gates_ref[...] = z


def _inproj(x2, g, w_in, cos, sin_signed, *, seq, tm):
    n = x2.shape[0]
    n_tiles = n // tm
    tiles_per_seq = seq // tm
    kb_per_tile = tm // MOBA_BLOCK
    n_kb = n // MOBA_BLOCK
    act = _MXU_DTYPE
    out_shape = (
        jax.ShapeDtypeStruct((n, 3 * D_MODEL), F32),
        jax.ShapeDtypeStruct((n_kb, D_MODEL, MOBA_BLOCK), act),
        jax.ShapeDtypeStruct((n_kb, MOBA_BLOCK, D_MODEL), act),
        jax.ShapeDtypeStruct((n_kb, D_MODEL, MOBA_BLOCK), act),
        jax.ShapeDtypeStruct((n, 2 * D_MODEL), F32),
        jax.ShapeDtypeStruct((n_tiles, kb_per_tile, D_MODEL), F32),
    )
    row_tile = lambda i, j: (i, 0)
    blk3 = lambda i, j: (i, 0, 0)
    return pl.pallas_call(
        functools.partial(_inproj_kernel, tm=tm),
        out_shape=out_shape,
        grid=(n_tiles, IN_PROJ_CHUNKS),
        in_specs=[
            pl.BlockSpec((tm, D_MODEL), row_tile),
            pl.BlockSpec((1, D_MODEL), lambda i, j: (0, 0)),
            pl.BlockSpec((D_MODEL, D_MODEL), lambda i, j: (0, j)),
            pl.BlockSpec((tm, LANES), lambda i, j: (i % tiles_per_seq, 0)),
            pl.BlockSpec((tm, LANES), lambda i, j: (i % tiles_per_seq, 0)),
        ],
        out_specs=(
            pl.BlockSpec((tm, D_MODEL), lambda i, j: (i, jnp.minimum(j, 2))),
            pl.BlockSpec((kb_per_tile, D_MODEL, MOBA_BLOCK), blk3),
            pl.BlockSpec((kb_per_tile, MOBA_BLOCK, D_MODEL), blk3),
            pl.BlockSpec((kb_per_tile, D_MODEL, MOBA_BLOCK), blk3),
            pl.BlockSpec((tm, D_MODEL), lambda i, j: (i, jnp.clip(j - 6, 0, 1))),
            pl.BlockSpec((1, kb_per_tile, D_MODEL), blk3),
        ),
        scratch_shapes=[pltpu.VMEM((tm, D_MODEL), act)],
        compiler_params=pltpu.CompilerParams(
            dimension_semantics=("arbitrary", "arbitrary"), vmem_limit_bytes=VMEM_LIMIT),
        name="inproj",
    )(x2, g, w_in, cos, sin_signed)


def _moba_kernel(qt_ref, k_ref, vt_ref, kmean_ref, onehot_ref, o_ref, *, n_kb):
    qb = pl.program_id(2)
    act = qt_ref.dtype
    blk = MOBA_BLOCK
    qt = qt_ref[0]
    km = kmean_ref[0].astype(act)
    dim_row = lax.broadcasted_iota(jnp.int32, (LANES, 1), 0)
    kb_row = lax.broadcasted_iota(jnp.int32, (n_kb, 1), 0)
    n_bias_rows = LANES
    ones_rows = jnp.ones((2 * SUBLANES, blk), act)

    q_aug = []
    for h in range(HEADS_PER_GROUP):
        in_head = (dim_row >= h * HEAD_DIM) & (dim_row < (h + 1) * HEAD_DIM)
        qh = jnp.where(in_head, qt, jnp.zeros_like(qt))
        gate = _dot(km, qh)
        work = jnp.where(kb_row < qb, gate, -jnp.inf)
        sel = jnp.zeros(gate.shape, jnp.bool_)
        for _ in range(TOP_BLOCKS):
            top = jnp.max(work, axis=0, keepdims=True)
            first = jnp.min(jnp.where(work == top, kb_row, n_kb), axis=0, keepdims=True)
            hit = kb_row == first
            sel = sel | (hit & (top > -jnp.inf))
            work = jnp.where(hit, -jnp.inf, work)
        bias = jnp.where(sel, 0.0, MASKED).astype(act)
        pad = jnp.zeros((n_bias_rows - n_kb, blk), act)
        q_aug.append(jnp.concatenate([qh, bias, pad], axis=0))

    def scores(kb, one_hot):
        k_blk = k_ref[kb]
        return [_dot(jnp.concatenate([k_blk, one_hot], axis=1), q_aug[h])
                for h in range(HEADS_PER_GROUP)]

    def pv(kb, h, p):
        v_h = vt_ref[kb, h * HEAD_DIM:(h + 1) * HEAD_DIM, :]
        return _dot(jnp.concatenate([v_h, ones_rows], axis=0), p.astype(act))

    key_idx = lax.broadcasted_iota(jnp.int32, (blk, blk), 0)
    qry_idx = lax.broadcasted_iota(jnp.int32, (blk, blk), 1)
    causal = key_idx <= qry_idx
    no_bias = jnp.zeros((blk, LANES), act)
    carry = []
    for h, s in enumerate(scores(qb, no_bias)):
        s = jnp.where(causal, s, MASKED)
        m = jnp.max(s, axis=0, keepdims=True)
        carry += [m, pv(qb, h, jnp.exp(s - m))]

    def body(kb, carry):
        new = []
        for h, s in enumerate(scores(kb, onehot_ref[kb])):
            m, acc = carry[2 * h], carry[2 * h + 1]
            m_new = jnp.maximum(m, jnp.max(s, axis=0, keepdims=True))
            alpha = jnp.exp(m - m_new)
            new += [m_new, alpha * acc + pv(kb, h, jnp.exp(s - m_new))]
        return tuple(new)

    carry = lax.fori_loop(0, qb, body, tuple(carry))
    outs = []
    for h in range(HEADS_PER_GROUP):
        acc = carry[2 * h + 1]
        outs.append(acc[:HEAD_DIM] / acc[HEAD_DIM:HEAD_DIM + 1])
    o_ref[...] = jnp.concatenate(outs, axis=0).T.astype(o_ref.dtype)


def _moba(qt, k, vt, kmean, *, batch, n_kb):
    n = batch * n_kb * MOBA_BLOCK
    onehot = jnp.arange(LANES)[None, None, :] == jnp.arange(n_kb)[:, None, None]
    onehot = jnp.broadcast_to(onehot, (n_kb, MOBA_BLOCK, LANES)).astype(qt.dtype)
    return pl.pallas_call(
        functools.partial(_moba_kernel, n_kb=n_kb),
        out_shape=jax.ShapeDtypeStruct((n, D_MODEL), qt.dtype),
        grid=(batch, N_HEAD_GROUPS, n_kb),
        in_specs=[
            pl.BlockSpec((1, LANES, MOBA_BLOCK), lambda b, g, q: (b * n_kb + q, g, 0)),
            pl.BlockSpec((n_kb, MOBA_BLOCK, LANES), lambda b, g, q: (b, 0, g)),
            pl.BlockSpec((n_kb, LANES, MOBA_BLOCK), lambda b, g, q: (b, g, 0)),
            pl.BlockSpec((1, n_kb, LANES), lambda b, g, q: (b, 0, g)),
            pl.BlockSpec((n_kb, MOBA_BLOCK, LANES), lambda b, g, q: (0, 0, 0)),
        ],
        out_specs=pl.BlockSpec((MOBA_BLOCK, LANES), lambda b, g, q: (b * n_kb + q, g)),
        compiler_params=pltpu.CompilerParams(
            dimension_semantics=("arbitrary", "arbitrary", "arbitrary"), vmem_limit_bytes=VMEM_LIMIT),
        name="moba",
    )(qt, k, vt, kmean, onehot)


def _mix_kernel(cb_ref, cc_ref, cx_ref, cch_ref, cxh_ref, gc_ref, ga_ref, attn_ref, x_ref,
                convw_ref, wco_ref, wao_ref, wo_ref, g_ref, rw_ref, rb_ref, tri_ref,
                h_ref, hn_ref, route_ref, prob_ref, counts_ref, carry_ref, *, tm, tiles_per_seq):
    i = pl.program_id(0)
    act = wco_ref.dtype

    @pl.when(i == 0)
    def _():
        carry_ref[...] = jnp.zeros_like(carry_ref)

    u = cc_ref[...] * cx_ref[...]
    keep = (i % tiles_per_seq != 0).astype(F32)
    halo = cch_ref[...] * cxh_ref[...] * keep
    prev1, prev2 = halo[SUBLANES - 1:SUBLANES, :], halo[SUBLANES - 2:SUBLANES - 1, :]
    row = lax.broadcasted_iota(jnp.int32, (tm, 1), 0)
    u1 = jnp.where(row == 0, prev1, pltpu.roll(u, 1, axis=0))
    u2 = jnp.where(row == 0, prev2, jnp.where(row == 1, prev1, pltpu.roll(u, 2, axis=0)))
    cw = convw_ref[...]
    conv = cw[0:1, :] * u2 + cw[1:2, :] * u1 + cw[2:3, :] * u
    y_conv = _dot((cb_ref[...] * conv).astype(act), wco_ref[...])
    y_attn = _dot(attn_ref[...], wao_ref[...])
    merged = _sigmoid(gc_ref[...]) * y_conv + _sigmoid(ga_ref[...]) * y_attn
    h = x_ref[...] + _dot(merged.astype(act), wo_ref[...])
    h_ref[...] = h
    hn = (h * lax.rsqrt(jnp.mean(h * h, axis=-1, keepdims=True) + EPS)) * g_ref[...]
    hn_ref[...] = hn

    logits = _dot(hn.astype(act), rw_ref[...]) + rb_ref[...]
    lane = lax.broadcasted_iota(jnp.int32, (1, LANES), 1)
    work = logits
    picks, vals = [], []
    sel = jnp.zeros(logits.shape, jnp.bool_)
    for _ in range(TOP_K):
        top = jnp.max(work, axis=-1, keepdims=True)
        first = jnp.min(jnp.where(work == top, lane, LANES), axis=-1, keepdims=True)
        hit = lane == first
        sel = sel | hit
        work = jnp.where(hit, -jnp.inf, work)
        picks.append(first)
        vals.append(top)
    exps = [jnp.exp(v - vals[0]) for v in vals]
    denom = exps[0] + exps[1] + exps[2] + exps[3]

    sel_f = jnp.where(sel, 1.0, 0.0)
    before = _dot(tri_ref[...], sel_f.astype(act)) + carry_ref[0:1, :]
    carry_ref[...] = carry_ref[...] + jnp.sum(sel_f, axis=0, keepdims=True)
    counts_ref[...] = carry_ref[...]

    route = jnp.zeros(logits.shape, jnp.int32)
    prob = jnp.zeros(logits.shape, F32)
    for t in range(TOP_K):
        rank = jnp.sum(jnp.where(lane == picks[t], before, 0.0), axis=-1, keepdims=True)
        route = jnp.where(lane == t, picks[t], route)
        route = jnp.where(lane == TOP_K + t, rank.astype(jnp.int32), route)
        prob = jnp.where(lane == t, exps[t] / denom, prob)
    route_ref[...] = route
    prob_ref[...] = prob


def _mix(zc, gates, attn, x2, conv_w, w_conv_out, w_attn_out, w_out, g, router_w, router_b, tri,
         *, seq, tm):
    n = x2.shape[0]
    n_tiles = n // tm
    halo_blocks = tm // SUBLANES
    row_tile = lambda i: (i, 0)
    const = lambda i: (0, 0)
    col = lambda c: (lambda i: (i, c))
    halo = lambda c: (lambda i: (jnp.maximum(i * halo_blocks - 1, 0), c))
    full = lambda a: pl.BlockSpec(a.shape, const)
    out_shape = (
        jax.ShapeDtypeStruct((n, D_MODEL), F32),
        jax.ShapeDtypeStruct((n, D_MODEL), F32),
        jax.ShapeDtypeStruct((n, LANES), jnp.int32),
        jax.ShapeDtypeStruct((n, LANES), F32),
        jax.ShapeDtypeStruct((SUBLANES, LANES), F32),
    )
    return pl.pallas_call(
        functools.partial(_mix_kernel, tm=tm, tiles_per_seq=seq // tm),
        out_shape=out_shape,
        grid=(n_tiles,),
        in_specs=[
            pl.BlockSpec((tm, D_MODEL), col(0)), pl.BlockSpec((tm, D_MODEL), col(1)),
            pl.BlockSpec((tm, D_MODEL), col(2)),
            pl.BlockSpec((SUBLANES, D_MODEL), halo(1)), pl.BlockSpec((SUBLANES, D_MODEL), halo(2)),
            pl.BlockSpec((tm, D_MODEL), col(0)), pl.BlockSpec((tm, D_MODEL), col(1)),
            pl.BlockSpec((tm, D_MODEL), row_tile), pl.BlockSpec((tm, D_MODEL), row_tile),
            full(conv_w), full(w_conv_out), full(w_attn_out), full(w_out), full(g),
            full(router_w), full(router_b), full(tri),
        ],
        out_specs=(
            pl.BlockSpec((tm, D_MODEL), row_tile), pl.BlockSpec((tm, D_MODEL), row_tile),
            pl.BlockSpec((tm, LANES), row_tile), pl.BlockSpec((tm, LANES), row_tile),
            pl.BlockSpec((SUBLANES, LANES), const),
        ),
        scratch_shapes=[pltpu.VMEM((SUBLANES, LANES), F32)],
        compiler_params=pltpu.CompilerParams(
            dimension_semantics=("arbitrary",), vmem_limit_bytes=VMEM_LIMIT),
        name="mix",
    )(zc, zc, zc, zc, zc, gates, gates, attn, x2, conv_w, w_conv_out, w_attn_out, w_out, g,
      router_w, router_b, tri)


def _row_copy(src_hbm, src_row, dst, dst_row, sem):
    return pltpu.make_async_copy(src_hbm.at[pl.ds(src_row, 1), :], dst.at[pl.ds(dst_row, 1), :], sem)


def _start_row_gather(src_hbm, dst, sem, tok_ref, n_rows):
    def issue(r, c):
        _row_copy(src_hbm, tok_ref[0, 0, r], dst, r, sem).start()
        return c
    lax.fori_loop(0, n_rows, issue, 0, unroll=8)


def _wait_row_gather(src_hbm, dst, sem, n_rows):
    def wait(r, c):
        _row_copy(src_hbm, 0, dst, r, sem).wait()
        return c
    lax.fori_loop(0, n_rows, wait, 0, unroll=8)


def _experts_kernel(be_ref, nused_ref, tok_ref, tok_next_ref, hn_hbm, wgu_ref, bgu_ref, wd_ref, bd_ref,
                    ys_ref, xbuf, wgu_act, wd_act, sem):
    i = pl.program_id(0)
    n_blocks = pl.num_programs(0)
    slot = i % 2
    act = wgu_act.dtype

    @pl.when(i == 0)
    def _():
        _start_row_gather(hn_hbm, xbuf.at[0], sem.at[0], tok_ref, MOE_BLOCK)

    @pl.when(i + 1 < n_blocks)
    def _():
        _start_row_gather(hn_hbm, xbuf.at[1 - slot], sem.at[1 - slot], tok_next_ref, MOE_BLOCK)

    @pl.when((i == 0) | (be_ref[i] != be_ref[jnp.maximum(i - 1, 0)]))
    def _():
        wgu_act[...] = wgu_ref[0].astype(act)
        wd_act[...] = wd_ref[0].astype(act)

    _wait_row_gather(hn_hbm, xbuf.at[slot], sem.at[slot], MOE_BLOCK)

    @pl.when(i < nused_ref[0])
    def _():
        gu = _dot(xbuf[slot].astype(act), wgu_act[...]) + bgu_ref[0]
        gate = jnp.minimum(gu[:, :D_FF], SWIGLU_LIMIT)
        up = jnp.clip(gu[:, D_FF:], -SWIGLU_LIMIT, SWIGLU_LIMIT)
        a = (up + 1.0) * (gate * _sigmoid(SWIGLU_ALPHA * gate))
        ys_ref[...] = _dot(a.astype(act), wd_act[...]) + bd_ref[0]

    @pl.when(i >= nused_ref[0])
    def _():
        ys_ref[...] = jnp.zeros_like(ys_ref)


def _experts(block_expert, n_used, slot_tok, hn, w_gate_up, b_gate_up, w_down, b_down):
    n_blocks = block_expert.shape[0]
    tok3 = slot_tok.reshape(n_blocks, 1, MOE_BLOCK)
    smem_blk = lambda f: pl.BlockSpec((1, 1, MOE_BLOCK), f, memory_space=pltpu.SMEM)
    grid_spec = pltpu.PrefetchScalarGridSpec(
        num_scalar_prefetch=2,
        grid=(n_blocks,),
        in_specs=[
            smem_blk(lambda i, be, nu: (i, 0, 0)),
            smem_blk(lambda i, be, nu: (jnp.minimum(i + 1, n_blocks - 1), 0, 0)),
            pl.BlockSpec(memory_space=pl.ANY),
            pl.BlockSpec((1, D_MODEL, 2 * D_FF), lambda i, be, nu: (be[i], 0, 0)),
            pl.BlockSpec((1, 1, 2 * D_FF), lambda i, be, nu: (be[i], 0, 0)),
            pl.BlockSpec((1, D_FF, D_MODEL), lambda i, be, nu: (be[i], 0, 0)),
            pl.BlockSpec((1, 1, D_MODEL), lambda i, be, nu: (be[i], 0, 0)),
        ],
        out_specs=pl.BlockSpec((MOE_BLOCK, D_MODEL), lambda i, be, nu: (i, 0)),
        scratch_shapes=[
            pltpu.VMEM((2, MOE_BLOCK, D_MODEL), F32),
            pltpu.VMEM((D_MODEL, 2 * D_FF), _MXU_DTYPE),
            pltpu.VMEM((D_FF, D_MODEL), _MXU_DTYPE),
            pltpu.SemaphoreType.DMA((2,)),
        ],
    )
    return pl.pallas_call(
        _experts_kernel,
        out_shape=jax.ShapeDtypeStruct((n_blocks * MOE_BLOCK, D_MODEL), F32),
        grid_spec=grid_spec,
        compiler_params=pltpu.CompilerParams(
            dimension_semantics=("arbitrary",), vmem_limit_bytes=VMEM_LIMIT),
        name="experts",
    )(block_expert, n_used, tok3, tok3, hn, w_gate_up,
      b_gate_up.reshape(N_EXPERTS, 1, 2 * D_FF), w_down, b_down.reshape(N_EXPERTS, 1, D_MODEL))


def _start_combine_gather(ys_hbm, ybuf, sem, dest_ref, tm):
    def issue(r, c):
        for k in range(TOP_K):
            _row_copy(ys_hbm, dest_ref[0, 0, r * TOP_K + k], ybuf.at[k], r, sem).start()
        return c
    lax.fori_loop(0, tm, issue, 0, unroll=4)


def _wait_combine_gather(ys_hbm, ybuf, sem, tm):
    def wait(r, c):
        for k in range(TOP_K):
            _row_copy(ys_hbm, 0, ybuf.at[k], r, sem).wait()
        return c
    lax.fori_loop(0, tm, wait, 0, unroll=4)


def _combine_kernel(dest_ref, dest_next_ref, ys_hbm, h_ref, prob_ref, g_ref, o_ref, ybuf, sem, *, tm):
    i = pl.program_id(0)
    n_tiles = pl.num_programs(0)
    slot = i % 2

    @pl.when(i == 0)
    def _():
        _start_combine_gather(ys_hbm, ybuf.at[0], sem.at[0], dest_ref, tm)

    @pl.when(i + 1 < n_tiles)
    def _():
        _start_combine_gather(ys_hbm, ybuf.at[1 - slot], sem.at[1 - slot], dest_next_ref, tm)

    _wait_combine_gather(ys_hbm, ybuf.at[slot], sem.at[slot], tm)
    prob = prob_ref[...]
    h = h_ref[...]
    for k in range(TOP_K):
        h = h + prob[:, k:k + 1] * ybuf[slot, k]
    o_ref[...] = (h * lax.rsqrt(jnp.mean(h * h, axis=-1, keepdims=True) + EPS)) * g_ref[...]


def _combine(dest, ys, h, prob, g, *, tm):
    n = h.shape[0]
    n_tiles = n // tm
    dest3 = dest.reshape(n_tiles, 1, tm * TOP_K)
    smem_blk = lambda f: pl.BlockSpec((1, 1, tm * TOP_K), f, memory_space=pltpu.SMEM)
    row_tile = lambda i: (i, 0)
    return pl.pallas_call(
        functools.partial(_combine_kernel, tm=tm),
        out_shape=jax.ShapeDtypeStruct((n, D_MODEL), F32),
        grid=(n_tiles,),
        in_specs=[
            smem_blk(lambda i: (i, 0, 0)),
            smem_blk(lambda i: (jnp.minimum(i + 1, n_tiles - 1), 0, 0)),
            pl.BlockSpec(memory_space=pl.ANY),
            pl.BlockSpec((tm, D_MODEL), row_tile),
            pl.BlockSpec((tm, LANES), row_tile),
            pl.BlockSpec((1, D_MODEL), lambda i: (0, 0)),
        ],
        out_specs=pl.BlockSpec((tm, D_MODEL), row_tile),
        scratch_shapes=[
            pltpu.VMEM((2, TOP_K, tm, D_MODEL), F32),
            pltpu.SemaphoreType.DMA((2,)),
        ],
        compiler_params=pltpu.CompilerParams(
            dimension_semantics=("arbitrary",), vmem_limit_bytes=VMEM_LIMIT),
        name="combine",
    )(dest3, dest3, ys, h, prob, g)


def _rope_tables(seq):
    inv_freq = ROPE_THETA ** (-jnp.arange(0, HALF_DIM, dtype=F32) / HALF_DIM)
    ang = jnp.arange(seq, dtype=jnp.int32).astype(F32)[:, None] * inv_freq[None, :]
    cos, sin = jnp.cos(ang), jnp.sin(ang)
    reps = LANES // HEAD_DIM
    cos_t = jnp.tile(jnp.concatenate([cos, cos], axis=1), (1, reps))
    sin_t = jnp.tile(jnp.concatenate([-sin, sin], axis=1), (1, reps))
    return cos_t, sin_t


def _tile_rows(seq, want):
    tm = min(want, seq)
    assert seq % tm == 0 and tm % MOBA_BLOCK == 0, (seq, tm)
    return tm


def kernel(x, norm_mix_g, w_in, conv_w, w_conv_out, w_attn_out, w_out, norm_ffn_g, router_w, router_b,
           w_gate_up, b_gate_up, w_down, b_down, norm_final_g):
    batch, seq, d = x.shape
    assert d == D_MODEL and seq % MOBA_BLOCK == 0
    n = batch * seq
    n_kb = seq // MOBA_BLOCK
    act = _MXU_DTYPE
    x2 = x.reshape(n, d)
    tm_proj = _tile_rows(seq, 512)
    tm_mix = _tile_rows(seq, 256)
    tm_out = _tile_rows(seq, 256)

    cos_t, sin_t = _rope_tables(seq)
    zc, qt, k, vt, gates, kmean = _inproj(
        x2, norm_mix_g.reshape(1, d), w_in.astype(act), cos_t, sin_t, seq=seq, tm=tm_proj)
    attn = _moba(qt, k, vt, kmean.reshape(batch, n_kb, d), batch=batch, n_kb=n_kb)

    router_w_pad = jnp.zeros((d, LANES), F32).at[:, :N_EXPERTS].set(router_w).astype(act)
    router_b_pad = jnp.full((1, LANES), MASKED, F32).at[0, :N_EXPERTS].set(router_b)
    tri = (jnp.arange(tm_mix)[:, None] > jnp.arange(tm_mix)[None, :]).astype(act)
    h, hn, route, prob, counts = _mix(
        zc, gates, attn, x2, conv_w, w_conv_out.astype(act), w_attn_out.astype(act), w_out.astype(act),
        norm_ffn_g.reshape(1, d), router_w_pad, router_b_pad, tri, seq=seq, tm=tm_mix)

    nk = n * TOP_K
    n_slots = ((nk + N_EXPERTS * (MOE_BLOCK - 1) + MOE_BLOCK - 1) // MOE_BLOCK) * MOE_BLOCK
    n_blocks = n_slots // MOE_BLOCK
    cnt = counts[0, :N_EXPERTS].astype(jnp.int32)
    padded = ((cnt + MOE_BLOCK - 1) // MOE_BLOCK) * MOE_BLOCK
    padded_end = jnp.cumsum(padded)
    start_pad = padded_end - padded
    expert = route[:, :TOP_K]
    rank = route[:, TOP_K:2 * TOP_K]
    dest = (start_pad[expert] + rank).reshape(nk)
    tok = jnp.repeat(jnp.arange(n, dtype=jnp.int32), TOP_K)
    slot_tok = jnp.zeros((n_slots,), jnp.int32).at[dest].set(tok)
    block_expert = jnp.clip(
        jnp.searchsorted(padded_end, jnp.arange(n_blocks, dtype=jnp.int32) * MOE_BLOCK, side="right"),
        0, N_EXPERTS - 1).astype(jnp.int32)
    n_used = (padded_end[-1:] // MOE_BLOCK).astype(jnp.int32)

    ys = _experts(block_expert, n_used, slot_tok, hn, w_gate_up, b_gate_up, w_down, b_down)
    out = _combine(dest, ys, h, prob, norm_final_g.reshape(1, d), tm=tm_out)
    return out.reshape(batch, seq, d)
```

```python
import functools

import jax
import jax.numpy as jnp
from jax import lax
from jax.experimental import pallas as pl
from jax.experimental.pallas import tpu as pltpu

D_MODEL = 1024
N_HEADS = 16
HEAD_DIM = 64
HALF_DIM = HEAD_DIM // 2
CONV_WIDTH = 3
MOBA_BLOCK = 256
TOP_BLOCKS = 3
ROPE_THETA = 10000.0
N_EXPERTS = 32
TOP_K = 4
D_FF = D_MODEL
SWIGLU_LIMIT = 7.0
SWIGLU_ALPHA = 1.702
MOE_BLOCK = 128
EPS = 1e-5
IN_PROJ_CHUNKS = 8

LANES = 128
SUBLANES = 8
HEADS_PER_GROUP = LANES // HEAD_DIM
N_HEAD_GROUPS = N_HEADS // HEADS_PER_GROUP
MOBA_GROUP = 4
MAX_LAGGED_RISE = 60.0
MASKED = -1e30
VMEM_LIMIT = 56 * 1024 * 1024

_MXU_DTYPE = jnp.bfloat16
F32 = jnp.float32


def _dot(a, b):
    return jnp.dot(a, b, preferred_element_type=F32)


def _sigmoid(t):
    return 1.0 / (1.0 + jnp.exp(-t))


def _rope_cols(z, cos, sin_signed):
    lane = lax.broadcasted_iota(jnp.int32, (1, LANES), 1)
    first_half = (lane % HEAD_DIM) < HALF_DIM
    cols = []
    for g in range(D_MODEL // LANES):
        t = z[:, g * LANES:(g + 1) * LANES]
        from_above = pltpu.roll(t, LANES - HALF_DIM, axis=1)
        from_below = pltpu.roll(t, HALF_DIM, axis=1)
        cols.append(t * cos + jnp.where(first_half, from_above, from_below) * sin_signed)
    return jnp.concatenate(cols, axis=1)


def _inproj_kernel(x_ref, g_ref, w_ref, cos_ref, sin_ref,
                   zc_ref, qt_ref, k_ref, vt_ref, gates_ref, kmean_ref, xn_ref, *, tm):
    j = pl.program_id(1)
    kb_per_tile = tm // MOBA_BLOCK

    @pl.when(j == 0)
    def _():
        x = x_ref[...]
        ms = jnp.mean(x * x, axis=-1, keepdims=True)
        xn_ref[...] = ((x * lax.rsqrt(ms + EPS)) * g_ref[...]).astype(xn_ref.dtype)

    z = _dot(xn_ref[...], w_ref[...])

    @pl.when(j < 3)
    def _():
        zc_ref[...] = z

    @pl.when(j == 3)
    def _():
        qt = (_rope_cols(z, cos_ref[...], sin_ref[...]) * (HEAD_DIM ** -0.5)).T
        for c in range(kb_per_tile):
            qt_ref[c] = qt[:, c * MOBA_BLOCK:(c + 1) * MOBA_BLOCK].astype(qt_ref.dtype)

    @pl.when(j == 4)
    def _():
        kr = _rope_cols(z, cos_ref[...], sin_ref[...])
        for c in range(kb_per_tile):
            blk = kr[c * MOBA_BLOCK:(c + 1) * MOBA_BLOCK, :]
            k_ref[c] = blk.astype(k_ref.dtype)
            kmean_ref[0, c:c + 1, :] = jnp.mean(blk, axis=0, keepdims=True)

    @pl.when(j == 5)
    def _():
        vt = z.T
        for c in range(kb_per_tile):
            vt_ref[c] = vt[:, c * MOBA_BLOCK:(c + 1) * MOBA_BLOCK].astype(vt_ref.dtype)

    @pl.when(j >= 6)
    def _():
        gates_ref[...] = z


def _inproj(x2, g, w_in, cos, sin_signed, *, seq, tm):
    n = x2.shape[0]
    n_tiles = n // tm
    tiles_per_seq = seq // tm
    kb_per_tile = tm // MOBA_BLOCK
    n_kb = n // MOBA_BLOCK
    act = _MXU_DTYPE
    out_shape = (
        jax.ShapeDtypeStruct((n, 3 * D_MODEL), F32),
        jax.ShapeDtypeStruct((n_kb, D_MODEL, MOBA_BLOCK), act),
        jax.ShapeDtypeStruct((n_kb, MOBA_BLOCK, D_MODEL), act),
        jax.ShapeDtypeStruct((n_kb, D_MODEL, MOBA_BLOCK), act),
        jax.ShapeDtypeStruct((n, 2 * D_MODEL), F32),
        jax.ShapeDtypeStruct((n_tiles, kb_per_tile, D_MODEL), F32),
    )
    row_tile = lambda i, j: (i, 0)
    blk3 = lambda i, j: (i, 0, 0)
    return pl.pallas_call(
        functools.partial(_inproj_kernel, tm=tm),
        out_shape=out_shape,
        grid=(n_tiles, IN_PROJ_CHUNKS),
        in_specs=[
            pl.BlockSpec((tm, D_MODEL), row_tile),
            pl.BlockSpec((1, D_MODEL), lambda i, j: (0, 0)),
            pl.BlockSpec((D_MODEL, D_MODEL), lambda i, j: (0, j)),
            pl.BlockSpec((tm, LANES), lambda i, j: (i % tiles_per_seq, 0)),
            pl.BlockSpec((tm, LANES), lambda i, j: (i % tiles_per_seq, 0)),
        ],
        out_specs=(
            pl.BlockSpec((tm, D_MODEL), lambda i, j: (i, jnp.minimum(j, 2))),
            pl.BlockSpec((kb_per_tile, D_MODEL, MOBA_BLOCK), blk3),
            pl.BlockSpec((kb_per_tile, MOBA_BLOCK, D_MODEL), blk3),
            pl.BlockSpec((kb_per_tile, D_MODEL, MOBA_BLOCK), blk3),
            pl.BlockSpec((tm, D_MODEL), lambda i, j: (i, jnp.clip(j - 6, 0, 1))),
            pl.BlockSpec((1, kb_per_tile, D_MODEL), blk3),
        ),
        scratch_shapes=[pltpu.VMEM((tm, D_MODEL), act)],
        compiler_params=pltpu.CompilerParams(
            dimension_semantics=("arbitrary", "arbitrary"), vmem_limit_bytes=VMEM_LIMIT),
        name="inproj",
    )(x2, g, w_in, cos, sin_signed)


def _moba_kernel(qt_ref, k_ref, vt_ref, kmean_ref, onehot_ref, o_ref, qaug_ref, s_ref, p_ref, *, n_kb, group):
    qb = pl.program_id(2)
    act = qt_ref.dtype
    blk = MOBA_BLOCK
    qt = qt_ref[0]
    km = kmean_ref[0].astype(act)
    dim_row = lax.broadcasted_iota(jnp.int32, (LANES, 1), 0)
    kb_row = lax.broadcasted_iota(jnp.int32, (n_kb, 1), 0)
    n_bias_rows = LANES

    q_aug = []
    for h in range(HEADS_PER_GROUP):
        in_head = (dim_row >= h * HEAD_DIM) & (dim_row < (h + 1) * HEAD_DIM)
        qh = jnp.where(in_head, qt, jnp.zeros_like(qt))
        gate = _dot(km, qh)
        work = jnp.where(kb_row < qb, gate, -jnp.inf)
        sel = jnp.zeros(gate.shape, jnp.bool_)
        for _ in range(TOP_BLOCKS):
            top = jnp.max(work, axis=0, keepdims=True)
            first = jnp.min(jnp.where(work == top, kb_row, n_kb), axis=0, keepdims=True)
            hit = kb_row == first
            sel = sel | (hit & (top > -jnp.inf))
            work = jnp.where(hit, -jnp.inf, work)
        bias = jnp.where(sel, 0.0, MASKED).astype(act)
        pad = jnp.zeros((n_bias_rows - n_kb, blk), act)
        q_aug.append(jnp.concatenate([qh, bias, pad], axis=0))

    for h in range(HEADS_PER_GROUP):
        qaug_ref[h] = q_aug[h]

    def v_rows(kb, h):
        return vt_ref[kb, h * HEAD_DIM:(h + 1) * HEAD_DIM, :]

    key_idx = lax.broadcasted_iota(jnp.int32, (blk, blk), 0)
    qry_idx = lax.broadcasted_iota(jnp.int32, (blk, blk), 1)
    causal = key_idx <= qry_idx
    k_own = jnp.concatenate([k_ref[qb], jnp.zeros((blk, LANES), act)], axis=1)
    ones_own = jnp.ones((2 * SUBLANES, blk), act)
    own = []
    for h in range(HEADS_PER_GROUP):
        s = jnp.where(causal, _dot(k_own, qaug_ref[h]), MASKED)
        m = jnp.max(s, axis=0, keepdims=True)
        own += [m, _dot(jnp.concatenate([v_rows(qb, h), ones_own], axis=0), jnp.exp(s - m).astype(act))]

    rows = group * blk
    n_groups = (qb + group - 1) // group
    ones_rows = jnp.ones((2 * SUBLANES, rows), act)

    def operands(j):
        kb0 = pl.multiple_of(j * group, group)
        k_big = k_ref[pl.ds(kb0, group)].reshape(rows, LANES)
        pick = onehot_ref[pl.ds(kb0, group)].reshape(rows, LANES)
        v_aug = [jnp.concatenate(
            [jnp.concatenate([v_rows(kb0 + u, h) for u in range(group)], axis=1), ones_rows], axis=0)
            for h in range(HEADS_PER_GROUP)]
        return jnp.concatenate([k_big, pick], axis=1), v_aug

    def write_out(accs):
        outs = [acc[:HEAD_DIM] / acc[HEAD_DIM:HEAD_DIM + 1] for acc in accs]
        o_ref[...] = jnp.concatenate(outs, axis=0).T.astype(o_ref.dtype)

    def lagged_body(j, carry):
        k_aug, v_aug = operands(j)
        heads = range(HEADS_PER_GROUP)
        s = [_dot(k_aug, qaug_ref[h]) for h in heads]
        top = []
        for h in heads:
            p_ref[h] = jnp.exp(s[h] - carry[3 * h]).astype(act)
            top.append(jnp.max(s[h], axis=0, keepdims=True))
        pv = [_dot(v_aug[h], p_ref[h]) for h in heads]
        new = []
        for h in heads:
            m, acc, rise = carry[3 * h:3 * h + 3]
            m_new = jnp.maximum(m, top[h])
            new += [m_new, (acc + pv[h]) * jnp.exp(m - m_new), jnp.maximum(rise, top[h] - m)]
        return tuple(new)

    init = []
    for h in range(HEADS_PER_GROUP):
        init += [own[2 * h], own[2 * h + 1], jnp.full((1, blk), MASKED, F32)]
    fast = lax.fori_loop(0, n_groups, lagged_body, tuple(init))
    write_out([fast[3 * h + 1] for h in range(HEADS_PER_GROUP)])
    worst_rise = jnp.max(functools.reduce(jnp.maximum, [fast[3 * h + 2] for h in range(HEADS_PER_GROUP)]))

    @pl.when(worst_rise > MAX_LAGGED_RISE)
    def _():
        def exact_body(j, carry):
            k_aug, v_aug = operands(j)
            new = []
            for h in range(HEADS_PER_GROUP):
                m, acc = carry[2 * h], carry[2 * h + 1]
                s_ref[h] = _dot(k_aug, qaug_ref[h])
                m_new = jnp.maximum(m, jnp.max(s_ref[h], axis=0, keepdims=True))
                p_ref[h] = jnp.exp(s_ref[h] - m_new).astype(act)
                new += [m_new, jnp.exp(m - m_new) * acc + _dot(v_aug[h], p_ref[h])]
            return tuple(new)

        safe = lax.fori_loop(0, n_groups, exact_body, tuple(own))
        write_out([safe[2 * h + 1] for h in range(HEADS_PER_GROUP)])


def _moba(qt, k, vt, kmean, *, batch, n_kb):
    n = batch * n_kb * MOBA_BLOCK
    group = MOBA_GROUP if n_kb % MOBA_GROUP == 0 else 1
    onehot = jnp.arange(LANES)[None, None, :] == jnp.arange(n_kb)[:, None, None]
    onehot = jnp.broadcast_to(onehot, (n_kb, MOBA_BLOCK, LANES)).astype(qt.dtype)
    return pl.pallas_call(
        functools.partial(_moba_kernel, n_kb=n_kb, group=group),
        out_shape=jax.ShapeDtypeStruct((n, D_MODEL), qt.dtype),
        grid=(batch, N_HEAD_GROUPS, n_kb),
        in_specs=[
            pl.BlockSpec((1, LANES, MOBA_BLOCK), lambda b, g, q: (b * n_kb + q, g, 0)),
            pl.BlockSpec((n_kb, MOBA_BLOCK, LANES), lambda b, g, q: (b, 0, g)),
            pl.BlockSpec((n_kb, LANES, MOBA_BLOCK), lambda b, g, q: (b, g, 0)),
            pl.BlockSpec((1, n_kb, LANES), lambda b, g, q: (b, 0, g)),
            pl.BlockSpec((n_kb, MOBA_BLOCK, LANES), lambda b, g, q: (0, 0, 0)),
        ],
        out_specs=pl.BlockSpec((MOBA_BLOCK, LANES), lambda b, g, q: (b * n_kb + q, g)),
        scratch_shapes=[
            pltpu.VMEM((HEADS_PER_GROUP, 2 * LANES, MOBA_BLOCK), qt.dtype),
            pltpu.VMEM((HEADS_PER_GROUP, group * MOBA_BLOCK, MOBA_BLOCK), F32),
            pltpu.VMEM((HEADS_PER_GROUP, group * MOBA_BLOCK, MOBA_BLOCK), qt.dtype),
        ],
        compiler_params=pltpu.CompilerParams(
            dimension_semantics=("arbitrary", "arbitrary", "arbitrary"), vmem_limit_bytes=VMEM_LIMIT),
        name="moba",
    )(qt, k, vt, kmean, onehot)


def _mix_kernel(cb_ref, cc_ref, cx_ref, cch_ref, cxh_ref, gc_ref, ga_ref, attn_ref, x_ref,
                convw_ref, wco_ref, wao_ref, wo_ref, g_ref, rw_ref, rb_ref, tri_ref,
                h_ref, hn_ref, route_ref, prob_ref, counts_ref, carry_ref, *, tm, tiles_per_seq):
    i = pl.program_id(0)
    act = wco_ref.dtype

    @pl.when(i == 0)
    def _():
        carry_ref[...] = jnp.zeros_like(carry_ref)

    u = cc_ref[...] * cx_ref[...]
    keep = (i % tiles_per_seq != 0).astype(F32)
    halo = cch_ref[...] * cxh_ref[...] * keep
    prev1, prev2 = halo[SUBLANES - 1:SUBLANES, :], halo[SUBLANES - 2:SUBLANES - 1, :]
    row = lax.broadcasted_iota(jnp.int32, (tm, 1), 0)
    u1 = jnp.where(row == 0, prev1, pltpu.roll(u, 1, axis=0))
    u2 = jnp.where(row == 0, prev2, jnp.where(row == 1, prev1, pltpu.roll(u, 2, axis=0)))
    cw = convw_ref[...]
    conv = cw[0:1, :] * u2 + cw[1:2, :] * u1 + cw[2:3, :] * u
    y_conv = _dot((cb_ref[...] * conv).astype(act), wco_ref[...])
    y_attn = _dot(attn_ref[...], wao_ref[...])
    merged = _sigmoid(gc_ref[...]) * y_conv + _sigmoid(ga_ref[...]) * y_attn
    h = x_ref[...] + _dot(merged.astype(act), wo_ref[...])
    h_ref[...] = h
    hn = (h * lax.rsqrt(jnp.mean(h * h, axis=-1, keepdims=True) + EPS)) * g_ref[...]
    hn_ref[...] = hn

    logits = _dot(hn.astype(act), rw_ref[...]) + rb_ref[...]
    lane = lax.broadcasted_iota(jnp.int32, (1, LANES), 1)
    work = logits
    picks, vals = [], []
    sel = jnp.zeros(logits.shape, jnp.bool_)
    for _ in range(TOP_K):
        top = jnp.max(work, axis=-1, keepdims=True)
        first = jnp.min(jnp.where(work == top, lane, LANES), axis=-1, keepdims=True)
        hit = lane == first
        sel = sel | hit
        work = jnp.where(hit, -jnp.inf, work)
        picks.append(first)
        vals.append(top)
    exps = [jnp.exp(v - vals[0]) for v in vals]
    denom = exps[0] + exps[1] + exps[2] + exps[3]

    sel_f = jnp.where(sel, 1.0, 0.0)
    before = _dot(tri_ref[...], sel_f.astype(act)) + carry_ref[0:1, :]
    carry_ref[...] = carry_ref[...] + jnp.sum(sel_f, axis=0, keepdims=True)
    counts_ref[...] = carry_ref[...]

    route = jnp.zeros(logits.shape, jnp.int32)
    prob = jnp.zeros(logits.shape, F32)
    for t in range(TOP_K):
        rank = jnp.sum(jnp.where(lane == picks[t], before, 0.0), axis=-1, keepdims=True)
        route = jnp.where(lane == t, picks[t], route)
        route = jnp.where(lane == TOP_K + t, rank.astype(jnp.int32), route)
        prob = jnp.where(lane == t, exps[t] / denom, prob)
    route_ref[...] = route
    prob_ref[...] = prob


def _mix(zc, gates, attn, x2, conv_w, w_conv_out, w_attn_out, w_out, g, router_w, router_b, tri,
         *, seq, tm):
    n = x2.shape[0]
    n_tiles = n // tm
    halo_blocks = tm // SUBLANES
    row_tile = lambda i: (i, 0)
    const = lambda i: (0, 0)
    col = lambda c: (lambda i: (i, c))
    halo = lambda c: (lambda i: (jnp.maximum(i * halo_blocks - 1, 0), c))
    full = lambda a: pl.BlockSpec(a.shape, const)
    out_shape = (
        jax.ShapeDtypeStruct((n, D_MODEL), F32),
        jax.ShapeDtypeStruct((n, D_MODEL), F32),
        jax.ShapeDtypeStruct((n, LANES), jnp.int32),
        jax.ShapeDtypeStruct((n, LANES), F32),
        jax.ShapeDtypeStruct((SUBLANES, LANES), F32),
    )
    return pl.pallas_call(
        functools.partial(_mix_kernel, tm=tm, tiles_per_seq=seq // tm),
        out_shape=out_shape,
        grid=(n_tiles,),
        in_specs=[
            pl.BlockSpec((tm, D_MODEL), col(0)), pl.BlockSpec((tm, D_MODEL), col(1)),
            pl.BlockSpec((tm, D_MODEL), col(2)),
            pl.BlockSpec((SUBLANES, D_MODEL), halo(1)), pl.BlockSpec((SUBLANES, D_MODEL), halo(2)),
            pl.BlockSpec((tm, D_MODEL), col(0)), pl.BlockSpec((tm, D_MODEL), col(1)),
            pl.BlockSpec((tm, D_MODEL), row_tile), pl.BlockSpec((tm, D_MODEL), row_tile),
            full(conv_w), full(w_conv_out), full(w_attn_out), full(w_out), full(g),
            full(router_w), full(router_b), full(tri),
        ],
        out_specs=(
            pl.BlockSpec((tm, D_MODEL), row_tile), pl.BlockSpec((tm, D_MODEL), row_tile),
            pl.BlockSpec((tm, LANES), row_tile), pl.BlockSpec((tm, LANES), row_tile),
            pl.BlockSpec((SUBLANES, LANES), const),
        ),
        scratch_shapes=[pltpu.VMEM((SUBLANES, LANES), F32)],
        compiler_params=pltpu.CompilerParams(
            dimension_semantics=("arbitrary",), vmem_limit_bytes=VMEM_LIMIT),
        name="mix",
    )(zc, zc, zc, zc, zc, gates, gates, attn, x2, conv_w, w_conv_out, w_attn_out, w_out, g,
      router_w, router_b, tri)


def _row_copy(src_hbm, src_row, dst, dst_row, sem):
    return pltpu.make_async_copy(src_hbm.at[pl.ds(src_row, 1), :], dst.at[pl.ds(dst_row, 1), :], sem)


def _start_row_gather(src_hbm, dst, sem, tok_ref, n_rows):
    def issue(r, c):
        _row_copy(src_hbm, tok_ref[0, 0, r], dst, r, sem).start()
        return c
    lax.fori_loop(0, n_rows, issue, 0, unroll=8)


def _wait_row_gather(src_hbm, dst, sem, n_rows):
    def wait(r, c):
        _row_copy(src_hbm, 0, dst, r, sem).wait()
        return c
    lax.fori_loop(0, n_rows, wait, 0, unroll=8)


def _experts_kernel(be_ref, nused_ref, tok_ref, tok_next_ref, hn_hbm, wgu_ref, bgu_ref, wd_ref, bd_ref,
                    ys_ref, xbuf, wgu_act, wd_act, sem):
    i = pl.program_id(0)
    n_blocks = pl.num_programs(0)
    slot = i % 2
    act = wgu_act.dtype

    @pl.when(i == 0)
    def _():
        _start_row_gather(hn_hbm, xbuf.at[0], sem.at[0], tok_ref, MOE_BLOCK)

    @pl.when(i + 1 < n_blocks)
    def _():
        _start_row_gather(hn_hbm, xbuf.at[1 - slot], sem.at[1 - slot], tok_next_ref, MOE_BLOCK)

    @pl.when((i == 0) | (be_ref[i] != be_ref[jnp.maximum(i - 1, 0)]))
    def _():
        wgu_act[...] = wgu_ref[0].astype(act)
        wd_act[...] = wd_ref[0].astype(act)

    _wait_row_gather(hn_hbm, xbuf.at[slot], sem.at[slot], MOE_BLOCK)

    @pl.when(i < nused_ref[0])
    def _():
        gu = _dot(xbuf[slot].astype(act), wgu_act[...]) + bgu_ref[0]
        gate = jnp.minimum(gu[:, :D_FF], SWIGLU_LIMIT)
        up = jnp.clip(gu[:, D_FF:], -SWIGLU_LIMIT, SWIGLU_LIMIT)
        a = (up + 1.0) * (gate * _sigmoid(SWIGLU_ALPHA * gate))
        ys_ref[...] = _dot(a.astype(act), wd_act[...]) + bd_ref[0]

    @pl.when(i >= nused_ref[0])
    def _():
        ys_ref[...] = jnp.zeros_like(ys_ref)


def _experts(block_expert, n_used, slot_tok, hn, w_gate_up, b_gate_up, w_down, b_down):
    n_blocks = block_expert.shape[0]
    tok3 = slot_tok.reshape(n_blocks, 1, MOE_BLOCK)
    smem_blk = lambda f: pl.BlockSpec((1, 1, MOE_BLOCK), f, memory_space=pltpu.SMEM)
    grid_spec = pltpu.PrefetchScalarGridSpec(
        num_scalar_prefetch=2,
        grid=(n_blocks,),
        in_specs=[
            smem_blk(lambda i, be, nu: (i, 0, 0)),
            smem_blk(lambda i, be, nu: (jnp.minimum(i + 1, n_blocks - 1), 0, 0)),
            pl.BlockSpec(memory_space=pl.ANY),
            pl.BlockSpec((1, D_MODEL, 2 * D_FF), lambda i, be, nu: (be[i], 0, 0)),
            pl.BlockSpec((1, 1, 2 * D_FF), lambda i, be, nu: (be[i], 0, 0)),
            pl.BlockSpec((1, D_FF, D_MODEL), lambda i, be, nu: (be[i], 0, 0)),
            pl.BlockSpec((1, 1, D_MODEL), lambda i, be, nu: (be[i], 0, 0)),
        ],
        out_specs=pl.BlockSpec((MOE_BLOCK, D_MODEL), lambda i, be, nu: (i, 0)),
        scratch_shapes=[
            pltpu.VMEM((2, MOE_BLOCK, D_MODEL), F32),
            pltpu.VMEM((D_MODEL, 2 * D_FF), _MXU_DTYPE),
            pltpu.VMEM((D_FF, D_MODEL), _MXU_DTYPE),
            pltpu.SemaphoreType.DMA((2,)),
        ],
    )
    return pl.pallas_call(
        _experts_kernel,
        out_shape=jax.ShapeDtypeStruct((n_blocks * MOE_BLOCK, D_MODEL), F32),
        grid_spec=grid_spec,
        compiler_params=pltpu.CompilerParams(
            dimension_semantics=("arbitrary",), vmem_limit_bytes=VMEM_LIMIT),
        name="experts",
    )(block_expert, n_used, tok3, tok3, hn, w_gate_up,
      b_gate_up.reshape(N_EXPERTS, 1, 2 * D_FF), w_down, b_down.reshape(N_EXPERTS, 1, D_MODEL))


def _start_combine_gather(ys_hbm, ybuf, sem, dest_ref, tm):
    def issue(r, c):
        for k in range(TOP_K):
            _row_copy(ys_hbm, dest_ref[0, 0, r * TOP_K + k], ybuf.at[k], r, sem).start()
        return c
    lax.fori_loop(0, tm, issue, 0, unroll=4)


def _wait_combine_gather(ys_hbm, ybuf, sem, tm):
    def wait(r, c):
        for k in range(TOP_K):
            _row_copy(ys_hbm, 0, ybuf.at[k], r, sem).wait()
        return c
    lax.fori_loop(0, tm, wait, 0, unroll=4)


def _combine_kernel(dest_ref, dest_next_ref, ys_hbm, h_ref, prob_ref, g_ref, o_ref, ybuf, sem, *, tm):
    i = pl.program_id(0)
    n_tiles = pl.num_programs(0)
    slot = i % 2

    @pl.when(i == 0)
    def _():
        _start_combine_gather(ys_hbm, ybuf.at[0], sem.at[0], dest_ref, tm)

    @pl.when(i + 1 < n_tiles)
    def _():
        _start_combine_gather(ys_hbm, ybuf.at[1 - slot], sem.at[1 - slot], dest_next_ref, tm)

    _wait_combine_gather(ys_hbm, ybuf.at[slot], sem.at[slot], tm)
    prob = prob_ref[...]
    h = h_ref[...]
    for k in range(TOP_K):
        h = h + prob[:, k:k + 1] * ybuf[slot, k]
    o_ref[...] = (h * lax.rsqrt(jnp.mean(h * h, axis=-1, keepdims=True) + EPS)) * g_ref[...]


def _combine(dest, ys, h, prob, g, *, tm):
    n = h.shape[0]
    n_tiles = n // tm
    dest3 = dest.reshape(n_tiles, 1, tm * TOP_K)
    smem_blk = lambda f: pl.BlockSpec((1, 1, tm * TOP_K), f, memory_space=pltpu.SMEM)
    row_tile = lambda i: (i, 0)
    return pl.pallas_call(
        functools.partial(_combine_kernel, tm=tm),
        out_shape=jax.ShapeDtypeStruct((n, D_MODEL), F32),
        grid=(n_tiles,),
        in_specs=[
            smem_blk(lambda i: (i, 0, 0)),
            smem_blk(lambda i: (jnp.minimum(i + 1, n_tiles - 1), 0, 0)),
            pl.BlockSpec(memory_space=pl.ANY),
            pl.BlockSpec((tm, D_MODEL), row_tile),
            pl.BlockSpec((tm, LANES), row_tile),
            pl.BlockSpec((1, D_MODEL), lambda i: (0, 0)),
        ],
        out_specs=pl.BlockSpec((tm, D_MODEL), row_tile),
        scratch_shapes=[
            pltpu.VMEM((2, TOP_K, tm, D_MODEL), F32),
            pltpu.SemaphoreType.DMA((2,)),
        ],
        compiler_params=pltpu.CompilerParams(
            dimension_semantics=("arbitrary",), vmem_limit_bytes=VMEM_LIMIT),
        name="combine",
    )(dest3, dest3, ys, h, prob, g)


def _rope_tables(seq):
    inv_freq = ROPE_THETA ** (-jnp.arange(0, HALF_DIM, dtype=F32) / HALF_DIM)
    ang = jnp.arange(seq, dtype=jnp.int32).astype(F32)[:, None] * inv_freq[None, :]
    cos, sin = jnp.cos(ang), jnp.sin(ang)
    reps = LANES // HEAD_DIM
    cos_t = jnp.tile(jnp.concatenate([cos, cos], axis=1), (1, reps))
    sin_t = jnp.tile(jnp.concatenate([-sin, sin], axis=1), (1, reps))
    return cos_t, sin_t


def _tile_rows(seq, want):
    tm = min(want, seq)
    assert seq % tm == 0 and tm % MOBA_BLOCK == 0, (seq, tm)
    return tm


def kernel(x, norm_mix_g, w_in, conv_w, w_conv_out, w_attn_out, w_out, norm_ffn_g, router_w, router_b,
           w_gate_up, b_gate_up, w_down, b_down, norm_final_g):
    batch, seq, d = x.shape
    assert d == D_MODEL and seq % MOBA_BLOCK == 0
    n = batch * seq
    n_kb = seq // MOBA_BLOCK
    act = _MXU_DTYPE
    x2 = x.reshape(n, d)
    tm_proj = _tile_rows(seq, 512)
    tm_mix = _tile_rows(seq, 256)
    tm_out = _tile_rows(seq, 256)

    cos_t, sin_t = _rope_tables(seq)
    zc, qt, k, vt, gates, kmean = _inproj(
        x2, norm_mix_g.reshape(1, d), w_in.astype(act), cos_t, sin_t, seq=seq, tm=tm_proj)
    attn = _moba(qt, k, vt, kmean.reshape(batch, n_kb, d), batch=batch, n_kb=n_kb)

    router_w_pad = jnp.zeros((d, LANES), F32).at[:, :N_EXPERTS].set(router_w).astype(act)
    router_b_pad = jnp.full((1, LANES), MASKED, F32).at[0, :N_EXPERTS].set(router_b)
    tri = (jnp.arange(tm_mix)[:, None] > jnp.arange(tm_mix)[None, :]).astype(act)
    h, hn, route, prob, counts = _mix(
        zc, gates, attn, x2, conv_w, w_conv_out.astype(act), w_attn_out.astype(act), w_out.astype(act),
        norm_ffn_g.reshape(1, d), router_w_pad, router_b_pad, tri, seq=seq, tm=tm_mix)

    nk = n * TOP_K
    n_slots = ((nk + N_EXPERTS * (MOE_BLOCK - 1) + MOE_BLOCK - 1) // MOE_BLOCK) * MOE_BLOCK
    n_blocks = n_slots // MOE_BLOCK
    cnt = counts[0, :N_EXPERTS].astype(jnp.int32)
    padded = ((cnt + MOE_BLOCK - 1) // MOE_BLOCK) * MOE_BLOCK
    padded_end = jnp.cumsum(padded)
    start_pad = padded_end - padded
    expert = route[:, :TOP_K]
    rank = route[:, TOP_K:2 * TOP_K]
    dest = (start_pad[expert] + rank).reshape(nk)
    tok = jnp.repeat(jnp.arange(n, dtype=jnp.int32), TOP_K)
    slot_tok = jnp.zeros((n_slots,), jnp.int32).at[dest].set(tok)
    block_start = jnp.arange(n_blocks, dtype=jnp.int32) * MOE_BLOCK
    block_expert = jnp.minimum(
        jnp.sum((padded_end[None, :] <= block_start[:, None]).astype(jnp.int32), axis=1), N_EXPERTS - 1)
    n_used = (padded_end[-1:] // MOE_BLOCK).astype(jnp.int32)

    ys = _experts(block_expert, n_used, slot_tok, hn, w_gate_up, b_gate_up, w_down, b_down)
    out = _combine(dest, ys, h, prob, norm_final_g.reshape(1, d), tm=tm_out)
    return out.reshape(batch, seq, d)
```

```python
import functools

import jax
import jax.numpy as jnp
from jax import lax
from jax.experimental import pallas as pl
from jax.experimental.pallas import tpu as pltpu

D_MODEL = 1024
N_HEADS = 16
HEAD_DIM = 64
HALF_DIM = HEAD_DIM // 2
CONV_WIDTH = 3
MOBA_BLOCK = 256
TOP_BLOCKS = 3
ROPE_THETA = 10000.0
N_EXPERTS = 32
TOP_K = 4
D_FF = D_MODEL
SWIGLU_LIMIT = 7.0
SWIGLU_ALPHA = 1.702
MOE_BLOCK = 128
EPS = 1e-5
IN_PROJ_CHUNKS = 8

LANES = 128
SUBLANES = 8
HEADS_PER_GROUP = LANES // HEAD_DIM
N_HEAD_GROUPS = N_HEADS // HEADS_PER_GROUP
MOBA_GROUP = 4
LOG2_E = 1.4426950408889634
MAX_LAGGED_RISE = 60.0
MASKED = -1e30
VMEM_LIMIT = 56 * 1024 * 1024

_MXU_DTYPE = jnp.bfloat16
F32 = jnp.float32


def _dot(a, b):
    return jnp.dot(a, b, preferred_element_type=F32)


def _sigmoid(t):
    return 1.0 / (1.0 + jnp.exp(-t))


def _rope_cols(z, cos, sin_signed):
    lane = lax.broadcasted_iota(jnp.int32, (1, LANES), 1)
    first_half = (lane % HEAD_DIM) < HALF_DIM
    cols = []
    for g in range(D_MODEL // LANES):
        t = z[:, g * LANES:(g + 1) * LANES]
        from_above = pltpu.roll(t, LANES - HALF_DIM, axis=1)
        from_below = pltpu.roll(t, HALF_DIM, axis=1)
        cols.append(t * cos + jnp.where(first_half, from_above, from_below) * sin_signed)
    return jnp.concatenate(cols, axis=1)


def _inproj_kernel(x_ref, g_ref, w_ref, cos_ref, sin_ref,
                   zc_ref, qt_ref, k_ref, vt_ref, gates_ref, kmean_ref, xn_ref, *, tm):
    j = pl.program_id(1)
    kb_per_tile = tm // MOBA_BLOCK

    @pl.when(j == 0)
    def _():
        x = x_ref[...]
        ms = jnp.mean(x * x, axis=-1, keepdims=True)
        xn_ref[...] = ((x * lax.rsqrt(ms + EPS)) * g_ref[...]).astype(xn_ref.dtype)

    z = _dot(xn_ref[...], w_ref[...])

    @pl.when(j < 3)
    def _():
        zc_ref[...] = z

    @pl.when(j == 3)
    def _():
        qt = (_rope_cols(z, cos_ref[...], sin_ref[...]) * (LOG2_E * HEAD_DIM ** -0.5)).T
        for c in range(kb_per_tile):
            qt_ref[c] = qt[:, c * MOBA_BLOCK:(c + 1) * MOBA_BLOCK].astype(qt_ref.dtype)

    @pl.when(j == 4)
    def _():
        kr = _rope_cols(z, cos_ref[...], sin_ref[...])
        for c in range(kb_per_tile):
            blk = kr[c * MOBA_BLOCK:(c + 1) * MOBA_BLOCK, :]
            k_ref[c] = blk.astype(k_ref.dtype)
            kmean_ref[0, c:c + 1, :] = jnp.mean(blk, axis=0, keepdims=True)

    @pl.when(j == 5)
    def _():
        vt = z.T
        for c in range(kb_per_tile):
            vt_ref[c] = vt[:, c * MOBA_BLOCK:(c + 1) * MOBA_BLOCK].astype(vt_ref.dtype)

    @pl.when(j >= 6)
    def _():
        gates_ref[...] = z


def _inproj(x2, g, w_in, cos, sin_signed, *, seq, tm):
    n = x2.shape[0]
    n_tiles = n // tm
    tiles_per_seq = seq // tm
    kb_per_tile = tm // MOBA_BLOCK
    n_kb = n // MOBA_BLOCK
    act = _MXU_DTYPE
    out_shape = (
        jax.ShapeDtypeStruct((n, 3 * D_MODEL), F32),
        jax.ShapeDtypeStruct((n_kb, D_MODEL, MOBA_BLOCK), act),
        jax.ShapeDtypeStruct((n_kb, MOBA_BLOCK, D_MODEL), act),
        jax.ShapeDtypeStruct((n_kb, D_MODEL, MOBA_BLOCK), act),
        jax.ShapeDtypeStruct((n, 2 * D_MODEL), F32),
        jax.ShapeDtypeStruct((n_tiles, kb_per_tile, D_MODEL), F32),
    )
    row_tile = lambda i, j: (i, 0)
    blk3 = lambda i, j: (i, 0, 0)
    return pl.pallas_call(
        functools.partial(_inproj_kernel, tm=tm),
        out_shape=out_shape,
        grid=(n_tiles, IN_PROJ_CHUNKS),
        in_specs=[
            pl.BlockSpec((tm, D_MODEL), row_tile),
            pl.BlockSpec((1, D_MODEL), lambda i, j: (0, 0)),
            pl.BlockSpec((D_MODEL, D_MODEL), lambda i, j: (0, j)),
            pl.BlockSpec((tm, LANES), lambda i, j: (i % tiles_per_seq, 0)),
            pl.BlockSpec((tm, LANES), lambda i, j: (i % tiles_per_seq, 0)),
        ],
        out_specs=(
            pl.BlockSpec((tm, D_MODEL), lambda i, j: (i, jnp.minimum(j, 2))),
            pl.BlockSpec((kb_per_tile, D_MODEL, MOBA_BLOCK), blk3),
            pl.BlockSpec((kb_per_tile, MOBA_BLOCK, D_MODEL), blk3),
            pl.BlockSpec((kb_per_tile, D_MODEL, MOBA_BLOCK), blk3),
            pl.BlockSpec((tm, D_MODEL), lambda i, j: (i, jnp.clip(j - 6, 0, 1))),
            pl.BlockSpec((1, kb_per_tile, D_MODEL), blk3),
        ),
        scratch_shapes=[pltpu.VMEM((tm, D_MODEL), act)],
        compiler_params=pltpu.CompilerParams(
            dimension_semantics=("arbitrary", "arbitrary"), vmem_limit_bytes=VMEM_LIMIT),
        name="inproj",
    )(x2, g, w_in, cos, sin_signed)


def _moba_kernel(qt_ref, k_ref, vt_ref, kmean_ref, onehot_ref, o_ref, qaug_ref, s_ref, p_ref, *, n_kb, group):
    qb = pl.program_id(2)
    act = qt_ref.dtype
    blk = MOBA_BLOCK
    qt = qt_ref[0]
    km = kmean_ref[0].astype(act)
    dim_row = lax.broadcasted_iota(jnp.int32, (LANES, 1), 0)
    kb_row = lax.broadcasted_iota(jnp.int32, (n_kb, 1), 0)
    n_bias_rows = LANES

    q_aug = []
    for h in range(HEADS_PER_GROUP):
        in_head = (dim_row >= h * HEAD_DIM) & (dim_row < (h + 1) * HEAD_DIM)
        qh = jnp.where(in_head, qt, jnp.zeros_like(qt))
        gate = _dot(km, qh)
        work = jnp.where(kb_row < qb, gate, -jnp.inf)
        sel = jnp.zeros(gate.shape, jnp.bool_)
        for _ in range(TOP_BLOCKS):
            top = jnp.max(work, axis=0, keepdims=True)
            first = jnp.min(jnp.where(work == top, kb_row, n_kb), axis=0, keepdims=True)
            hit = kb_row == first
            sel = sel | (hit & (top > -jnp.inf))
            work = jnp.where(hit, -jnp.inf, work)
        bias = jnp.where(sel, 0.0, MASKED).astype(act)
        pad = jnp.zeros((n_bias_rows - n_kb, blk), act)
        q_aug.append(jnp.concatenate([qh, bias, pad], axis=0))

    for h in range(HEADS_PER_GROUP):
        qaug_ref[h] = q_aug[h]

    def v_rows(kb, h):
        return vt_ref[kb, h * HEAD_DIM:(h + 1) * HEAD_DIM, :]

    key_idx = lax.broadcasted_iota(jnp.int32, (blk, blk), 0)
    qry_idx = lax.broadcasted_iota(jnp.int32, (blk, blk), 1)
    causal = key_idx <= qry_idx
    k_own = jnp.concatenate([k_ref[qb], jnp.zeros((blk, LANES), act)], axis=1)
    ones_own = jnp.ones((2 * SUBLANES, blk), act)
    heads = range(HEADS_PER_GROUP)
    own = []
    for h in heads:
        s = jnp.where(causal, _dot(k_own, qaug_ref[h]), MASKED)
        m = jnp.max(s, axis=0, keepdims=True)
        own += [m, _dot(jnp.concatenate([v_rows(qb, h), ones_own], axis=0), jnp.exp2(s - m).astype(act))]

    rows = group * blk
    n_groups = jnp.maximum((qb + group - 1) // group, 1)
    ones_rows = jnp.ones((2 * SUBLANES, rows), act)

    def k_operand(j):
        kb0 = pl.multiple_of(j * group, group)
        k_big = k_ref[pl.ds(kb0, group)].reshape(rows, LANES)
        pick = onehot_ref[pl.ds(kb0, group)].reshape(rows, LANES)
        return jnp.concatenate([k_big, pick], axis=1)

    def v_operand(j, h):
        kb0 = pl.multiple_of(j * group, group)
        v_big = jnp.concatenate([v_rows(kb0 + u, h) for u in range(group)], axis=1)
        return jnp.concatenate([v_big, ones_rows], axis=0)

    def write_out(accs):
        outs = [acc[:HEAD_DIM] / acc[HEAD_DIM:HEAD_DIM + 1] for acc in accs]
        o_ref[...] = jnp.concatenate(outs, axis=0).T.astype(o_ref.dtype)

    def score_stage(j, refs, slot):
        s = [_dot(k_operand(j), qaug_ref[h]) for h in heads]
        tops = []
        for h in heads:
            p_ref[slot, h] = jnp.exp2(s[h] - refs[h]).astype(act)
            tops.append(jnp.max(s[h], axis=0, keepdims=True))
        return tops

    def pv_stage(j, slot):
        return [_dot(v_operand(j, h), p_ref[slot, h]) for h in heads]

    def pipelined_body(j, carry):
        pv = pv_stage(j - 1, (j - 1) % 2)
        tops = score_stage(j, [carry[4 * h + 1] for h in heads], j % 2)
        new = []
        for h in heads:
            ref_prev, ref, acc, rise = carry[4 * h:4 * h + 4]
            new += [ref, jnp.maximum(ref, tops[h]), (acc + pv[h]) * jnp.exp2(ref_prev - ref),
                    jnp.maximum(rise, tops[h] - ref)]
        return tuple(new)

    tops = score_stage(0, [own[2 * h] for h in heads], 0)
    init = []
    for h in heads:
        ref = own[2 * h]
        init += [ref, jnp.maximum(ref, tops[h]), own[2 * h + 1], tops[h] - ref]
    fast = lax.fori_loop(1, n_groups, pipelined_body, tuple(init))
    pv = pv_stage(n_groups - 1, (n_groups - 1) % 2)
    write_out([fast[4 * h + 2] + pv[h] for h in heads])
    worst_rise = jnp.max(functools.reduce(jnp.maximum, [fast[4 * h + 3] for h in heads]))

    @pl.when(worst_rise > MAX_LAGGED_RISE)
    def _():
        def exact_body(j, carry):
            k_aug = k_operand(j)
            new = []
            for h in heads:
                m, acc = carry[2 * h], carry[2 * h + 1]
                s_ref[h] = _dot(k_aug, qaug_ref[h])
                m_new = jnp.maximum(m, jnp.max(s_ref[h], axis=0, keepdims=True))
                p_ref[0, h] = jnp.exp2(s_ref[h] - m_new).astype(act)
                new += [m_new, jnp.exp2(m - m_new) * acc + _dot(v_operand(j, h), p_ref[0, h])]
            return tuple(new)

        safe = lax.fori_loop(0, n_groups, exact_body, tuple(own))
        write_out([safe[2 * h + 1] for h in heads])


def _moba(qt, k, vt, kmean, *, batch, n_kb):
    n = batch * n_kb * MOBA_BLOCK
    group = MOBA_GROUP if n_kb % MOBA_GROUP == 0 else 1
    onehot = jnp.arange(LANES)[None, None, :] == jnp.arange(n_kb)[:, None, None]
    onehot = jnp.broadcast_to(onehot, (n_kb, MOBA_BLOCK, LANES)).astype(qt.dtype)
    return pl.pallas_call(
        functools.partial(_moba_kernel, n_kb=n_kb, group=group),
        out_shape=jax.ShapeDtypeStruct((n, D_MODEL), qt.dtype),
        grid=(batch, N_HEAD_GROUPS, n_kb),
        in_specs=[
            pl.BlockSpec((1, LANES, MOBA_BLOCK), lambda b, g, q: (b * n_kb + q, g, 0)),
            pl.BlockSpec((n_kb, MOBA_BLOCK, LANES), lambda b, g, q: (b, 0, g)),
            pl.BlockSpec((n_kb, LANES, MOBA_BLOCK), lambda b, g, q: (b, g, 0)),
            pl.BlockSpec((1, n_kb, LANES), lambda b, g, q: (b, 0, g)),
            pl.BlockSpec((n_kb, MOBA_BLOCK, LANES), lambda b, g, q: (0, 0, 0)),
        ],
        out_specs=pl.BlockSpec((MOBA_BLOCK, LANES), lambda b, g, q: (b * n_kb + q, g)),
        scratch_shapes=[
            pltpu.VMEM((HEADS_PER_GROUP, 2 * LANES, MOBA_BLOCK), qt.dtype),
            pltpu.VMEM((HEADS_PER_GROUP, group * MOBA_BLOCK, MOBA_BLOCK), F32),
            pltpu.VMEM((2, HEADS_PER_GROUP, group * MOBA_BLOCK, MOBA_BLOCK), qt.dtype),
        ],
        compiler_params=pltpu.CompilerParams(
            dimension_semantics=("arbitrary", "arbitrary", "arbitrary"), vmem_limit_bytes=VMEM_LIMIT),
        name="moba",
    )(qt, k, vt, kmean, onehot)


def _mix_kernel(cb_ref, cc_ref, cx_ref, cch_ref, cxh_ref, gc_ref, ga_ref, attn_ref, x_ref,
                convw_ref, wco_ref, wao_ref, wo_ref, g_ref, rw_ref, rb_ref, tri_ref,
                h_ref, hn_ref, route_ref, prob_ref, counts_ref, carry_ref, *, tm, tiles_per_seq):
    i = pl.program_id(0)
    act = wco_ref.dtype

    @pl.when(i == 0)
    def _():
        carry_ref[...] = jnp.zeros_like(carry_ref)

    u = cc_ref[...] * cx_ref[...]
    keep = (i % tiles_per_seq != 0).astype(F32)
    halo = cch_ref[...] * cxh_ref[...] * keep
    prev1, prev2 = halo[SUBLANES - 1:SUBLANES, :], halo[SUBLANES - 2:SUBLANES - 1, :]
    row = lax.broadcasted_iota(jnp.int32, (tm, 1), 0)
    u1 = jnp.where(row == 0, prev1, pltpu.roll(u, 1, axis=0))
    u2 = jnp.where(row == 0, prev2, jnp.where(row == 1, prev1, pltpu.roll(u, 2, axis=0)))
    cw = convw_ref[...]
    conv = cw[0:1, :] * u2 + cw[1:2, :] * u1 + cw[2:3, :] * u
    y_conv = _dot((cb_ref[...] * conv).astype(act), wco_ref[...])
    y_attn = _dot(attn_ref[...], wao_ref[...])
    merged = _sigmoid(gc_ref[...]) * y_conv + _sigmoid(ga_ref[...]) * y_attn
    h = x_ref[...] + _dot(merged.astype(act), wo_ref[...])
    h_ref[...] = h
    hn = (h * lax.rsqrt(jnp.mean(h * h, axis=-1, keepdims=True) + EPS)) * g_ref[...]
    hn_ref[...] = hn

    logits = _dot(hn.astype(act), rw_ref[...]) + rb_ref[...]
    lane = lax.broadcasted_iota(jnp.int32, (1, LANES), 1)
    work = logits
    picks, vals = [], []
    sel = jnp.zeros(logits.shape, jnp.bool_)
    for _ in range(TOP_K):
        top = jnp.max(work, axis=-1, keepdims=True)
        first = jnp.min(jnp.where(work == top, lane, LANES), axis=-1, keepdims=True)
        hit = lane == first
        sel = sel | hit
        work = jnp.where(hit, -jnp.inf, work)
        picks.append(first)
        vals.append(top)
    exps = [jnp.exp(v - vals[0]) for v in vals]
    denom = exps[0] + exps[1] + exps[2] + exps[3]

    sel_f = jnp.where(sel, 1.0, 0.0)
    before = _dot(tri_ref[...], sel_f.astype(act)) + carry_ref[0:1, :]
    carry_ref[...] = carry_ref[...] + jnp.sum(sel_f, axis=0, keepdims=True)
    counts_ref[...] = carry_ref[...]

    route = jnp.zeros(logits.shape, jnp.int32)
    prob = jnp.zeros(logits.shape, F32)
    for t in range(TOP_K):
        rank = jnp.sum(jnp.where(lane == picks[t], before, 0.0), axis=-1, keepdims=True)
        route = jnp.where(lane == t, picks[t], route)
        route = jnp.where(lane == TOP_K + t, rank.astype(jnp.int32), route)
        prob = jnp.where(lane == t, exps[t] / denom, prob)
    route_ref[...] = route
    prob_ref[...] = prob


def _mix(zc, gates, attn, x2, conv_w, w_conv_out, w_attn_out, w_out, g, router_w, router_b, tri,
         *, seq, tm):
    n = x2.shape[0]
    n_tiles = n // tm
    halo_blocks = tm // SUBLANES
    row_tile = lambda i: (i, 0)
    const = lambda i: (0, 0)
    col = lambda c: (lambda i: (i, c))
    halo = lambda c: (lambda i: (jnp.maximum(i * halo_blocks - 1, 0), c))
    full = lambda a: pl.BlockSpec(a.shape, const)
    out_shape = (
        jax.ShapeDtypeStruct((n, D_MODEL), F32),
        jax.ShapeDtypeStruct((n, D_MODEL), F32),
        jax.ShapeDtypeStruct((n, LANES), jnp.int32),
        jax.ShapeDtypeStruct((n, LANES), F32),
        jax.ShapeDtypeStruct((SUBLANES, LANES), F32),
    )
    return pl.pallas_call(
        functools.partial(_mix_kernel, tm=tm, tiles_per_seq=seq // tm),
        out_shape=out_shape,
        grid=(n_tiles,),
        in_specs=[
            pl.BlockSpec((tm, D_MODEL), col(0)), pl.BlockSpec((tm, D_MODEL), col(1)),
            pl.BlockSpec((tm, D_MODEL), col(2)),
            pl.BlockSpec((SUBLANES, D_MODEL), halo(1)), pl.BlockSpec((SUBLANES, D_MODEL), halo(2)),
            pl.BlockSpec((tm, D_MODEL), col(0)), pl.BlockSpec((tm, D_MODEL), col(1)),
            pl.BlockSpec((tm, D_MODEL), row_tile), pl.BlockSpec((tm, D_MODEL), row_tile),
            full(conv_w), full(w_conv_out), full(w_attn_out), full(w_out), full(g),
            full(router_w), full(router_b), full(tri),
        ],
        out_specs=(
            pl.BlockSpec((tm, D_MODEL), row_tile), pl.BlockSpec((tm, D_MODEL), row_tile),
            pl.BlockSpec((tm, LANES), row_tile), pl.BlockSpec((tm, LANES), row_tile),
            pl.BlockSpec((SUBLANES, LANES), const),
        ),
        scratch_shapes=[pltpu.VMEM((SUBLANES, LANES), F32)],
        compiler_params=pltpu.CompilerParams(
            dimension_semantics=("arbitrary",), vmem_limit_bytes=VMEM_LIMIT),
        name="mix",
    )(zc, zc, zc, zc, zc, gates, gates, attn, x2, conv_w, w_conv_out, w_attn_out, w_out, g,
      router_w, router_b, tri)


def _row_copy(src, src_row, dst, dst_row, sem):
    return pltpu.make_async_copy(src.at[pl.ds(src_row, 1), :], dst.at[pl.ds(dst_row, 1), :], sem)


def _start_rows(copy_of_row, n_rows):
    for r in range(n_rows):
        copy_of_row(r).start()


def _wait_rows(copy_of_row, n_rows):
    def wait(r, c):
        copy_of_row(r).wait()
        return c
    lax.fori_loop(0, n_rows, wait, 0, unroll=8)


def _experts_kernel(be_ref, nused_ref, tok_ref, tok_next_ref, dst_prev_ref, dst_ref, hn_hbm,
                    wgu_ref, bgu_ref, wd_ref, bd_ref, y4_hbm, xbuf, ybuf, wgu_act, wd_act, gsem, ssem):
    i = pl.program_id(0)
    last = pl.num_programs(0) - 1
    slot = i % 2
    other = 1 - slot
    act = wgu_act.dtype
    n_used = nused_ref[0]

    def gather(toks, s):
        return lambda r: _row_copy(hn_hbm, 0 if toks is None else toks[0, 0, r], xbuf.at[s], r, gsem.at[s])

    def scatter(dsts, s):
        return lambda r: _row_copy(ybuf.at[s], r, y4_hbm, 0 if dsts is None else dsts[0, 0, r], ssem.at[s])

    @pl.when(i == 0)
    def _():
        _start_rows(gather(tok_ref, 0), MOE_BLOCK)

    @pl.when((i == 0) | (be_ref[i] != be_ref[jnp.maximum(i - 1, 0)]))
    def _():
        wgu_act[...] = wgu_ref[0].astype(act)
        wd_act[...] = wd_ref[0].astype(act)

    _wait_rows(gather(None, slot), MOE_BLOCK)

    @pl.when(i >= 2)
    def _():
        _wait_rows(scatter(None, slot), MOE_BLOCK)

    def step(compute, scatter_prev):
        x = xbuf[slot].astype(act) if compute else None
        _start_rows(gather(tok_next_ref, other), MOE_BLOCK)
        if scatter_prev:
            _start_rows(scatter(dst_prev_ref, other), MOE_BLOCK)
        if compute:
            gu = _dot(x, wgu_act[...]) + bgu_ref[0]
            gate = jnp.minimum(gu[:, :D_FF], SWIGLU_LIMIT)
            up = jnp.clip(gu[:, D_FF:], -SWIGLU_LIMIT, SWIGLU_LIMIT)
            a = (up + 1.0) * (gate * _sigmoid(SWIGLU_ALPHA * gate))
            ybuf[slot] = _dot(a.astype(act), wd_act[...]) + bd_ref[0]
        else:
            ybuf[slot] = jnp.zeros((MOE_BLOCK, D_MODEL), F32)

    pl.when((i < n_used) & (i > 0))(lambda: step(True, True))
    pl.when((i < n_used) & (i == 0))(lambda: step(True, False))
    pl.when(i >= n_used)(lambda: step(False, True))

    @pl.when(i == last)
    def _():
        _start_rows(scatter(dst_ref, slot), MOE_BLOCK)
        _wait_rows(scatter(None, other), MOE_BLOCK)
        _wait_rows(scatter(None, slot), MOE_BLOCK)
        _wait_rows(gather(None, other), MOE_BLOCK)


def _experts(block_expert, n_used, slot_tok, slot_dst, hn, w_gate_up, b_gate_up, w_down, b_down, *, y4_rows):
    n_blocks = block_expert.shape[0]
    assert n_blocks >= 2
    tok3 = slot_tok.reshape(n_blocks, 1, MOE_BLOCK)
    dst3 = slot_dst.reshape(n_blocks, 1, MOE_BLOCK)
    smem_blk = lambda f: pl.BlockSpec((1, 1, MOE_BLOCK), f, memory_space=pltpu.SMEM)
    grid_spec = pltpu.PrefetchScalarGridSpec(
        num_scalar_prefetch=2,
        grid=(n_blocks,),
        in_specs=[
            smem_blk(lambda i, be, nu: (i, 0, 0)),
            smem_blk(lambda i, be, nu: (jnp.minimum(i + 1, n_blocks - 1), 0, 0)),
            smem_blk(lambda i, be, nu: (jnp.maximum(i - 1, 0), 0, 0)),
            smem_blk(lambda i, be, nu: (i, 0, 0)),
            pl.BlockSpec(memory_space=pl.ANY),
            pl.BlockSpec((1, D_MODEL, 2 * D_FF), lambda i, be, nu: (be[i], 0, 0)),
            pl.BlockSpec((1, 1, 2 * D_FF), lambda i, be, nu: (be[i], 0, 0)),
            pl.BlockSpec((1, D_FF, D_MODEL), lambda i, be, nu: (be[i], 0, 0)),
            pl.BlockSpec((1, 1, D_MODEL), lambda i, be, nu: (be[i], 0, 0)),
        ],
        out_specs=pl.BlockSpec(memory_space=pl.ANY),
        scratch_shapes=[
            pltpu.VMEM((2, MOE_BLOCK, D_MODEL), F32),
            pltpu.VMEM((2, MOE_BLOCK, D_MODEL), F32),
            pltpu.VMEM((D_MODEL, 2 * D_FF), _MXU_DTYPE),
            pltpu.VMEM((D_FF, D_MODEL), _MXU_DTYPE),
            pltpu.SemaphoreType.DMA((2,)),
            pltpu.SemaphoreType.DMA((2,)),
        ],
    )
    return pl.pallas_call(
        _experts_kernel,
        out_shape=jax.ShapeDtypeStruct((y4_rows, D_MODEL), F32),
        grid_spec=grid_spec,
        compiler_params=pltpu.CompilerParams(
            dimension_semantics=("arbitrary",), vmem_limit_bytes=VMEM_LIMIT),
        name="experts",
    )(block_expert, n_used, tok3, tok3, dst3, dst3, hn, w_gate_up,
      b_gate_up.reshape(N_EXPERTS, 1, 2 * D_FF), w_down, b_down.reshape(N_EXPERTS, 1, D_MODEL))


def _combine_kernel(h_ref, prob_ref, g_ref, *refs):
    y_refs, o_ref = refs[:TOP_K], refs[TOP_K]
    prob = prob_ref[...]
    h = h_ref[...]
    for k in range(TOP_K):
        h = h + prob[:, k:k + 1] * y_refs[k][...]
    o_ref[...] = (h * lax.rsqrt(jnp.mean(h * h, axis=-1, keepdims=True) + EPS)) * g_ref[...]


def _combine(y4, h, prob, g, *, plane, tm):
    n = h.shape[0]
    assert plane % tm == 0
    row_tile = lambda i: (i, 0)
    plane_tile = lambda k: (lambda i: (k * (plane // tm) + i, 0))
    return pl.pallas_call(
        _combine_kernel,
        out_shape=jax.ShapeDtypeStruct((n, D_MODEL), F32),
        grid=(n // tm,),
        in_specs=[
            pl.BlockSpec((tm, D_MODEL), row_tile),
            pl.BlockSpec((tm, LANES), row_tile),
            pl.BlockSpec((1, D_MODEL), lambda i: (0, 0)),
        ] + [pl.BlockSpec((tm, D_MODEL), plane_tile(k)) for k in range(TOP_K)],
        out_specs=pl.BlockSpec((tm, D_MODEL), row_tile),
        compiler_params=pltpu.CompilerParams(
            dimension_semantics=("arbitrary",), vmem_limit_bytes=VMEM_LIMIT),
        name="combine",
    )(h, prob, g, *([y4] * TOP_K))


def _rope_tables(seq):
    inv_freq = ROPE_THETA ** (-jnp.arange(0, HALF_DIM, dtype=F32) / HALF_DIM)
    ang = jnp.arange(seq, dtype=jnp.int32).astype(F32)[:, None] * inv_freq[None, :]
    cos, sin = jnp.cos(ang), jnp.sin(ang)
    reps = LANES // HEAD_DIM
    cos_t = jnp.tile(jnp.concatenate([cos, cos], axis=1), (1, reps))
    sin_t = jnp.tile(jnp.concatenate([-sin, sin], axis=1), (1, reps))
    return cos_t, sin_t


def _tile_rows(seq, want):
    tm = min(want, seq)
    assert seq % tm == 0 and tm % MOBA_BLOCK == 0, (seq, tm)
    return tm


def kernel(x, norm_mix_g, w_in, conv_w, w_conv_out, w_attn_out, w_out, norm_ffn_g, router_w, router_b,
           w_gate_up, b_gate_up, w_down, b_down, norm_final_g):
    batch, seq, d = x.shape
    assert d == D_MODEL and seq % MOBA_BLOCK == 0
    n = batch * seq
    n_kb = seq // MOBA_BLOCK
    act = _MXU_DTYPE
    x2 = x.reshape(n, d)
    tm_proj = _tile_rows(seq, 512)
    tm_mix = _tile_rows(seq, 256)
    tm_out = _tile_rows(seq, 256)

    cos_t, sin_t = _rope_tables(seq)
    zc, qt, k, vt, gates, kmean = _inproj(
        x2, norm_mix_g.reshape(1, d), w_in.astype(act), cos_t, sin_t, seq=seq, tm=tm_proj)
    attn = _moba(qt, k, vt, kmean.reshape(batch, n_kb, d), batch=batch, n_kb=n_kb)

    router_w_pad = jnp.zeros((d, LANES), F32).at[:, :N_EXPERTS].set(router_w).astype(act)
    router_b_pad = jnp.full((1, LANES), MASKED, F32).at[0, :N_EXPERTS].set(router_b)
    tri = (jnp.arange(tm_mix)[:, None] > jnp.arange(tm_mix)[None, :]).astype(act)
    h, hn, route, prob, counts = _mix(
        zc, gates, attn, x2, conv_w, w_conv_out.astype(act), w_attn_out.astype(act), w_out.astype(act),
        norm_ffn_g.reshape(1, d), router_w_pad, router_b_pad, tri, seq=seq, tm=tm_mix)

    nk = n * TOP_K
    n_slots = ((nk + N_EXPERTS * (MOE_BLOCK - 1) + MOE_BLOCK - 1) // MOE_BLOCK) * MOE_BLOCK
    n_blocks = n_slots // MOE_BLOCK
    cnt = counts[0, :N_EXPERTS].astype(jnp.int32)
    padded = ((cnt + MOE_BLOCK - 1) // MOE_BLOCK) * MOE_BLOCK
    padded_end = jnp.cumsum(padded)
    start_pad = padded_end - padded
    expert = route[:, :TOP_K]
    rank = route[:, TOP_K:2 * TOP_K]
    dest = (start_pad[expert] + rank).reshape(nk)
    n_pad = n_slots - nk
    plane = n + n_pad // TOP_K
    slot_pair = jnp.full((n_slots,), -1, jnp.int32).at[dest].set(jnp.arange(nk, dtype=jnp.int32))
    routed = slot_pair >= 0
    pad_rank = jnp.cumsum(jnp.where(routed, 0, 1).astype(jnp.int32)) - 1
    slot_tok = jnp.where(routed, slot_pair // TOP_K, 0)
    slot_dst = jnp.where(routed, (slot_pair % TOP_K) * plane + slot_pair // TOP_K,
                         (pad_rank % TOP_K) * plane + n + pad_rank // TOP_K)
    block_start = jnp.arange(n_blocks, dtype=jnp.int32) * MOE_BLOCK
    block_expert = jnp.minimum(
        jnp.sum((padded_end[None, :] <= block_start[:, None]).astype(jnp.int32), axis=1), N_EXPERTS - 1)
    n_used = (padded_end[-1:] // MOE_BLOCK).astype(jnp.int32)

    y4 = _experts(block_expert, n_used, slot_tok, slot_dst, hn, w_gate_up, b_gate_up, w_down, b_down,
                  y4_rows=TOP_K * plane)
    out = _combine(y4, h, prob, norm_final_g.reshape(1, d), plane=plane, tm=tm_out)
    return out.reshape(batch, seq, d)
```

```python
import functools

import jax
import jax.numpy as jnp
from jax import lax
from jax.experimental import pallas as pl
from jax.experimental.pallas import tpu as pltpu

D_MODEL = 1024
N_HEADS = 16
HEAD_DIM = 64
HALF_DIM = HEAD_DIM // 2
CONV_WIDTH = 3
MOBA_BLOCK = 256
TOP_BLOCKS = 3
ROPE_THETA = 10000.0
N_EXPERTS = 32
TOP_K = 4
D_FF = D_MODEL
SWIGLU_LIMIT = 7.0
SWIGLU_ALPHA = 1.702
MOE_BLOCK = 128
EPS = 1e-5
IN_PROJ_CHUNKS = 8

LANES = 128
SUBLANES = 8
ROW_TILES = D_MODEL // LANES
HEADS_PER_GROUP = LANES // HEAD_DIM
N_HEAD_GROUPS = N_HEADS // HEADS_PER_GROUP
MOBA_GROUP = 4
LOG2_E = 1.4426950408889634
MAX_LAGGED_RISE = 60.0
MASKED = -1e30
VMEM_LIMIT = 56 * 1024 * 1024

_MXU_DTYPE = jnp.bfloat16
F32 = jnp.float32


def _dot(a, b):
    return jnp.dot(a, b, preferred_element_type=F32)


def _sigmoid(t):
    return 1.0 / (1.0 + jnp.exp(-t))


def _rope_cols(z, cos, sin_signed):
    lane = lax.broadcasted_iota(jnp.int32, (1, LANES), 1)
    first_half = (lane % HEAD_DIM) < HALF_DIM
    cols = []
    for g in range(D_MODEL // LANES):
        t = z[:, g * LANES:(g + 1) * LANES]
        from_above = pltpu.roll(t, LANES - HALF_DIM, axis=1)
        from_below = pltpu.roll(t, HALF_DIM, axis=1)
        cols.append(t * cos + jnp.where(first_half, from_above, from_below) * sin_signed)
    return jnp.concatenate(cols, axis=1)


def _inproj_kernel(x_ref, g_ref, w_ref, cos_ref, sin_ref,
                   zc_ref, qt_ref, k_ref, vt_ref, gates_ref, kmean_ref, xn_ref, *, tm):
    j = pl.program_id(1)
    kb_per_tile = tm // MOBA_BLOCK

    @pl.when(j == 0)
    def _():
        x = x_ref[...]
        ms = jnp.mean(x * x, axis=-1, keepdims=True)
        xn_ref[...] = ((x * lax.rsqrt(ms + EPS)) * g_ref[...]).astype(xn_ref.dtype)

    z = _dot(xn_ref[...], w_ref[...])

    @pl.when(j < 3)
    def _():
        zc_ref[...] = z

    @pl.when(j == 3)
    def _():
        qt = (_rope_cols(z, cos_ref[...], sin_ref[...]) * (LOG2_E * HEAD_DIM ** -0.5)).T
        for c in range(kb_per_tile):
            qt_ref[c] = qt[:, c * MOBA_BLOCK:(c + 1) * MOBA_BLOCK].astype(qt_ref.dtype)

    @pl.when(j == 4)
    def _():
        kr = _rope_cols(z, cos_ref[...], sin_ref[...])
        for c in range(kb_per_tile):
            blk = kr[c * MOBA_BLOCK:(c + 1) * MOBA_BLOCK, :]
            k_ref[c] = blk.astype(k_ref.dtype)
            kmean_ref[0, c:c + 1, :] = jnp.mean(blk, axis=0, keepdims=True)

    @pl.when(j == 5)
    def _():
        vt = z.T
        for c in range(kb_per_tile):
            vt_ref[c] = vt[:, c * MOBA_BLOCK:(c + 1) * MOBA_BLOCK].astype(vt_ref.dtype)

    @pl.when(j >= 6)
    def _():
        gates_ref[...] = z


def _inproj(x2, g, w_in, cos, sin_signed, *, seq, tm):
    n = x2.shape[0]
    n_tiles = n // tm
    tiles_per_seq = seq // tm
    kb_per_tile = tm // MOBA_BLOCK
    n_kb = n // MOBA_BLOCK
    act = _MXU_DTYPE
    out_shape = (
        jax.ShapeDtypeStruct((n, 3 * D_MODEL), F32),
        jax.ShapeDtypeStruct((n_kb, D_MODEL, MOBA_BLOCK), act),
        jax.ShapeDtypeStruct((n_kb, MOBA_BLOCK, D_MODEL), act),
        jax.ShapeDtypeStruct((n_kb, D_MODEL, MOBA_BLOCK), act),
        jax.ShapeDtypeStruct((n, 2 * D_MODEL), F32),
        jax.ShapeDtypeStruct((n_tiles, kb_per_tile, D_MODEL), F32),
    )
    row_tile = lambda i, j: (i, 0)
    blk3 = lambda i, j: (i, 0, 0)
    return pl.pallas_call(
        functools.partial(_inproj_kernel, tm=tm),
        out_shape=out_shape,
        grid=(n_tiles, IN_PROJ_CHUNKS),
        in_specs=[
            pl.BlockSpec((tm, D_MODEL), row_tile),
            pl.BlockSpec((1, D_MODEL), lambda i, j: (0, 0)),
            pl.BlockSpec((D_MODEL, D_MODEL), lambda i, j: (0, j)),
            pl.BlockSpec((tm, LANES), lambda i, j: (i % tiles_per_seq, 0)),
            pl.BlockSpec((tm, LANES), lambda i, j: (i % tiles_per_seq, 0)),
        ],
        out_specs=(
            pl.BlockSpec((tm, D_MODEL), lambda i, j: (i, jnp.minimum(j, 2))),
            pl.BlockSpec((kb_per_tile, D_MODEL, MOBA_BLOCK), blk3),
            pl.BlockSpec((kb_per_tile, MOBA_BLOCK, D_MODEL), blk3),
            pl.BlockSpec((kb_per_tile, D_MODEL, MOBA_BLOCK), blk3),
            pl.BlockSpec((tm, D_MODEL), lambda i, j: (i, jnp.clip(j - 6, 0, 1))),
            pl.BlockSpec((1, kb_per_tile, D_MODEL), blk3),
        ),
        scratch_shapes=[pltpu.VMEM((tm, D_MODEL), act)],
        compiler_params=pltpu.CompilerParams(
            dimension_semantics=("arbitrary", "arbitrary"), vmem_limit_bytes=VMEM_LIMIT),
        name="inproj",
    )(x2, g, w_in, cos, sin_signed)


def _moba_kernel(qt_ref, k_ref, vt_ref, kmean_ref, onehot_ref, o_ref, qaug_ref, s_ref, p_ref, *, n_kb, group):
    qb = pl.program_id(2)
    act = qt_ref.dtype
    blk = MOBA_BLOCK
    qt = qt_ref[0]
    km = kmean_ref[0].astype(act)
    dim_row = lax.broadcasted_iota(jnp.int32, (LANES, 1), 0)
    kb_row = lax.broadcasted_iota(jnp.int32, (n_kb, 1), 0)
    n_bias_rows = LANES

    q_aug = []
    for h in range(HEADS_PER_GROUP):
        in_head = (dim_row >= h * HEAD_DIM) & (dim_row < (h + 1) * HEAD_DIM)
        qh = jnp.where(in_head, qt, jnp.zeros_like(qt))
        gate = _dot(km, qh)
        work = jnp.where(kb_row < qb, gate, -jnp.inf)
        sel = jnp.zeros(gate.shape, jnp.bool_)
        for _ in range(TOP_BLOCKS):
            top = jnp.max(work, axis=0, keepdims=True)
            first = jnp.min(jnp.where(work == top, kb_row, n_kb), axis=0, keepdims=True)
            hit = kb_row == first
            sel = sel | (hit & (top > -jnp.inf))
            work = jnp.where(hit, -jnp.inf, work)
        bias = jnp.where(sel, 0.0, MASKED).astype(act)
        pad = jnp.zeros((n_bias_rows - n_kb, blk), act)
        q_aug.append(jnp.concatenate([qh, bias, pad], axis=0))

    for h in range(HEADS_PER_GROUP):
        qaug_ref[h] = q_aug[h]

    def v_rows(kb, h):
        return vt_ref[kb, h * HEAD_DIM:(h + 1) * HEAD_DIM, :]

    key_idx = lax.broadcasted_iota(jnp.int32, (blk, blk), 0)
    qry_idx = lax.broadcasted_iota(jnp.int32, (blk, blk), 1)
    causal = key_idx <= qry_idx
    k_own = jnp.concatenate([k_ref[qb], jnp.zeros((blk, LANES), act)], axis=1)
    ones_own = jnp.ones((2 * SUBLANES, blk), act)
    heads = range(HEADS_PER_GROUP)
    own = []
    for h in heads:
        s = jnp.where(causal, _dot(k_own, qaug_ref[h]), MASKED)
        m = jnp.max(s, axis=0, keepdims=True)
        own += [m, _dot(jnp.concatenate([v_rows(qb, h), ones_own], axis=0), jnp.exp2(s - m).astype(act))]

    rows = group * blk
    n_groups = jnp.maximum((qb + group - 1) // group, 1)
    ones_rows = jnp.ones((2 * SUBLANES, rows), act)

    def k_operand(j):
        kb0 = pl.multiple_of(j * group, group)
        k_big = k_ref[pl.ds(kb0, group)].reshape(rows, LANES)
        pick = onehot_ref[pl.ds(kb0, group)].reshape(rows, LANES)
        return jnp.concatenate([k_big, pick], axis=1)

    def v_operand(j, h):
        kb0 = pl.multiple_of(j * group, group)
        v_big = jnp.concatenate([v_rows(kb0 + u, h) for u in range(group)], axis=1)
        return jnp.concatenate([v_big, ones_rows], axis=0)

    def write_out(accs):
        outs = [acc[:HEAD_DIM] / acc[HEAD_DIM:HEAD_DIM + 1] for acc in accs]
        o_ref[...] = jnp.concatenate(outs, axis=0).T.astype(o_ref.dtype)

    def score_stage(j, refs, slot):
        s = [_dot(k_operand(j), qaug_ref[h]) for h in heads]
        tops = []
        for h in heads:
            p_ref[slot, h] = jnp.exp2(s[h] - refs[h]).astype(act)
            tops.append(jnp.max(s[h], axis=0, keepdims=True))
        return tops

    def pv_stage(j, slot):
        return [_dot(v_operand(j, h), p_ref[slot, h]) for h in heads]

    def pipelined_body(j, carry):
        pv = pv_stage(j - 1, (j - 1) % 2)
        tops = score_stage(j, [carry[4 * h + 1] for h in heads], j % 2)
        new = []
        for h in heads:
            ref_prev, ref, acc, rise = carry[4 * h:4 * h + 4]
            new += [ref, jnp.maximum(ref, tops[h]), (acc + pv[h]) * jnp.exp2(ref_prev - ref),
                    jnp.maximum(rise, tops[h] - ref)]
        return tuple(new)

    tops = score_stage(0, [own[2 * h] for h in heads], 0)
    init = []
    for h in heads:
        ref = own[2 * h]
        init += [ref, jnp.maximum(ref, tops[h]), own[2 * h + 1], tops[h] - ref]
    fast = lax.fori_loop(1, n_groups, pipelined_body, tuple(init))
    pv = pv_stage(n_groups - 1, (n_groups - 1) % 2)
    write_out([fast[4 * h + 2] + pv[h] for h in heads])
    worst_rise = jnp.max(functools.reduce(jnp.maximum, [fast[4 * h + 3] for h in heads]))

    @pl.when(worst_rise > MAX_LAGGED_RISE)
    def _():
        def exact_body(j, carry):
            k_aug = k_operand(j)
            new = []
            for h in heads:
                m, acc = carry[2 * h], carry[2 * h + 1]
                s_ref[h] = _dot(k_aug, qaug_ref[h])
                m_new = jnp.maximum(m, jnp.max(s_ref[h], axis=0, keepdims=True))
                p_ref[0, h] = jnp.exp2(s_ref[h] - m_new).astype(act)
                new += [m_new, jnp.exp2(m - m_new) * acc + _dot(v_operand(j, h), p_ref[0, h])]
            return tuple(new)

        safe = lax.fori_loop(0, n_groups, exact_body, tuple(own))
        write_out([safe[2 * h + 1] for h in heads])


def _moba(qt, k, vt, kmean, *, batch, n_kb):
    n = batch * n_kb * MOBA_BLOCK
    group = MOBA_GROUP if n_kb % MOBA_GROUP == 0 else 1
    onehot = jnp.arange(LANES)[None, None, :] == jnp.arange(n_kb)[:, None, None]
    onehot = jnp.broadcast_to(onehot, (n_kb, MOBA_BLOCK, LANES)).astype(qt.dtype)
    return pl.pallas_call(
        functools.partial(_moba_kernel, n_kb=n_kb, group=group),
        out_shape=jax.ShapeDtypeStruct((n, D_MODEL), qt.dtype),
        grid=(batch, N_HEAD_GROUPS, n_kb),
        in_specs=[
            pl.BlockSpec((1, LANES, MOBA_BLOCK), lambda b, g, q: (b * n_kb + q, g, 0)),
            pl.BlockSpec((n_kb, MOBA_BLOCK, LANES), lambda b, g, q: (b, 0, g)),
            pl.BlockSpec((n_kb, LANES, MOBA_BLOCK), lambda b, g, q: (b, g, 0)),
            pl.BlockSpec((1, n_kb, LANES), lambda b, g, q: (b, 0, g)),
            pl.BlockSpec((n_kb, MOBA_BLOCK, LANES), lambda b, g, q: (0, 0, 0)),
        ],
        out_specs=pl.BlockSpec((MOBA_BLOCK, LANES), lambda b, g, q: (b * n_kb + q, g)),
        scratch_shapes=[
            pltpu.VMEM((HEADS_PER_GROUP, 2 * LANES, MOBA_BLOCK), qt.dtype),
            pltpu.VMEM((HEADS_PER_GROUP, group * MOBA_BLOCK, MOBA_BLOCK), F32),
            pltpu.VMEM((2, HEADS_PER_GROUP, group * MOBA_BLOCK, MOBA_BLOCK), qt.dtype),
        ],
        compiler_params=pltpu.CompilerParams(
            dimension_semantics=("arbitrary", "arbitrary", "arbitrary"), vmem_limit_bytes=VMEM_LIMIT),
        name="moba",
    )(qt, k, vt, kmean, onehot)


def _mix_kernel(cb_ref, cc_ref, cx_ref, cch_ref, cxh_ref, gc_ref, ga_ref, attn_ref, x_ref,
                convw_ref, wco_ref, wao_ref, wo_ref, g_ref, rw_ref, rb_ref, tri_ref,
                h_ref, hn_ref, route_ref, prob_ref, counts_ref, carry_ref, *, tm, tiles_per_seq):
    i = pl.program_id(0)
    act = wco_ref.dtype

    @pl.when(i == 0)
    def _():
        carry_ref[...] = jnp.zeros_like(carry_ref)

    u = cc_ref[...] * cx_ref[...]
    keep = (i % tiles_per_seq != 0).astype(F32)
    halo = cch_ref[...] * cxh_ref[...] * keep
    prev1, prev2 = halo[SUBLANES - 1:SUBLANES, :], halo[SUBLANES - 2:SUBLANES - 1, :]
    row = lax.broadcasted_iota(jnp.int32, (tm, 1), 0)
    u1 = jnp.where(row == 0, prev1, pltpu.roll(u, 1, axis=0))
    u2 = jnp.where(row == 0, prev2, jnp.where(row == 1, prev1, pltpu.roll(u, 2, axis=0)))
    cw = convw_ref[...]
    conv = cw[0:1, :] * u2 + cw[1:2, :] * u1 + cw[2:3, :] * u
    y_conv = _dot((cb_ref[...] * conv).astype(act), wco_ref[...])
    y_attn = _dot(attn_ref[...], wao_ref[...])
    merged = _sigmoid(gc_ref[...]) * y_conv + _sigmoid(ga_ref[...]) * y_attn
    h = x_ref[...] + _dot(merged.astype(act), wo_ref[...])
    h_ref[...] = h
    hn = (h * lax.rsqrt(jnp.mean(h * h, axis=-1, keepdims=True) + EPS)) * g_ref[...]
    for c in range(ROW_TILES):
        hn_ref[:, c, :] = hn[:, c * LANES:(c + 1) * LANES]

    logits = _dot(hn.astype(act), rw_ref[...]) + rb_ref[...]
    lane = lax.broadcasted_iota(jnp.int32, (1, LANES), 1)
    work = logits
    picks, vals = [], []
    sel = jnp.zeros(logits.shape, jnp.bool_)
    for _ in range(TOP_K):
        top = jnp.max(work, axis=-1, keepdims=True)
        first = jnp.min(jnp.where(work == top, lane, LANES), axis=-1, keepdims=True)
        hit = lane == first
        sel = sel | hit
        work = jnp.where(hit, -jnp.inf, work)
        picks.append(first)
        vals.append(top)
    exps = [jnp.exp(v - vals[0]) for v in vals]
    denom = exps[0] + exps[1] + exps[2] + exps[3]

    sel_f = jnp.where(sel, 1.0, 0.0)
    before = _dot(tri_ref[...], sel_f.astype(act)) + carry_ref[0:1, :]
    carry_ref[...] = carry_ref[...] + jnp.sum(sel_f, axis=0, keepdims=True)
    counts_ref[...] = carry_ref[...]

    route = jnp.zeros(logits.shape, jnp.int32)
    prob = jnp.zeros(logits.shape, F32)
    for t in range(TOP_K):
        rank = jnp.sum(jnp.where(lane == picks[t], before, 0.0), axis=-1, keepdims=True)
        route = jnp.where(lane == t, picks[t], route)
        route = jnp.where(lane == TOP_K + t, rank.astype(jnp.int32), route)
        prob = jnp.where(lane == t, exps[t] / denom, prob)
    route_ref[...] = route
    prob_ref[...] = prob


def _mix(zc, gates, attn, x2, conv_w, w_conv_out, w_attn_out, w_out, g, router_w, router_b, tri,
         *, seq, tm):
    n = x2.shape[0]
    n_tiles = n // tm
    halo_blocks = tm // SUBLANES
    row_tile = lambda i: (i, 0)
    const = lambda i: (0, 0)
    col = lambda c: (lambda i: (i, c))
    halo = lambda c: (lambda i: (jnp.maximum(i * halo_blocks - 1, 0), c))
    full = lambda a: pl.BlockSpec(a.shape, const)
    out_shape = (
        jax.ShapeDtypeStruct((n, D_MODEL), F32),
        jax.ShapeDtypeStruct((n, ROW_TILES, LANES), F32),
        jax.ShapeDtypeStruct((n, LANES), jnp.int32),
        jax.ShapeDtypeStruct((n, LANES), F32),
        jax.ShapeDtypeStruct((SUBLANES, LANES), F32),
    )
    return pl.pallas_call(
        functools.partial(_mix_kernel, tm=tm, tiles_per_seq=seq // tm),
        out_shape=out_shape,
        grid=(n_tiles,),
        in_specs=[
            pl.BlockSpec((tm, D_MODEL), col(0)), pl.BlockSpec((tm, D_MODEL), col(1)),
            pl.BlockSpec((tm, D_MODEL), col(2)),
            pl.BlockSpec((SUBLANES, D_MODEL), halo(1)), pl.BlockSpec((SUBLANES, D_MODEL), halo(2)),
            pl.BlockSpec((tm, D_MODEL), col(0)), pl.BlockSpec((tm, D_MODEL), col(1)),
            pl.BlockSpec((tm, D_MODEL), row_tile), pl.BlockSpec((tm, D_MODEL), row_tile),
            full(conv_w), full(w_conv_out), full(w_attn_out), full(w_out), full(g),
            full(router_w), full(router_b), full(tri),
        ],
        out_specs=(
            pl.BlockSpec((tm, D_MODEL), row_tile), pl.BlockSpec((tm, ROW_TILES, LANES), lambda i: (i, 0, 0)),
            pl.BlockSpec((tm, LANES), row_tile), pl.BlockSpec((tm, LANES), row_tile),
            pl.BlockSpec((SUBLANES, LANES), const),
        ),
        scratch_shapes=[pltpu.VMEM((SUBLANES, LANES), F32)],
        compiler_params=pltpu.CompilerParams(
            dimension_semantics=("arbitrary",), vmem_limit_bytes=VMEM_LIMIT),
        name="mix",
    )(zc, zc, zc, zc, zc, gates, gates, attn, x2, conv_w, w_conv_out, w_attn_out, w_out, g,
      router_w, router_b, tri)


def _row_copy(src, src_row, dst, dst_row, sem):
    return pltpu.make_async_copy(src.at[pl.ds(src_row, 1)], dst.at[pl.ds(dst_row, 1)], sem)


def _slabs_to_rows(ref_slice):
    return jnp.concatenate([ref_slice[:, c, :] for c in range(ROW_TILES)], axis=1)


def _start_rows(copy_of_row, n_rows):
    for r in range(n_rows):
        copy_of_row(r).start()


def _wait_rows(copy_of_row, n_rows):
    def wait(r, c):
        copy_of_row(r).wait()
        return c
    lax.fori_loop(0, n_rows, wait, 0, unroll=8)


def _experts_kernel(be_ref, nused_ref, tok_ref, tok_next_ref, dst_prev_ref, dst_ref, hn_hbm,
                    wgu_ref, bgu_ref, wd_ref, bd_ref, y4_hbm, xbuf, ybuf, wgu_act, wd_act, gsem, ssem):
    i = pl.program_id(0)
    last = pl.num_programs(0) - 1
    slot = i % 2
    other = 1 - slot
    act = wgu_act.dtype
    n_used = nused_ref[0]

    def gather(toks, s):
        return lambda r: _row_copy(hn_hbm, 0 if toks is None else toks[0, 0, r], xbuf.at[s], r, gsem.at[s])

    def scatter(dsts, s):
        return lambda r: _row_copy(ybuf.at[s], r, y4_hbm, 0 if dsts is None else dsts[0, 0, r], ssem.at[s])

    @pl.when(i == 0)
    def _():
        _start_rows(gather(tok_ref, 0), MOE_BLOCK)

    @pl.when((i == 0) | (be_ref[i] != be_ref[jnp.maximum(i - 1, 0)]))
    def _():
        wgu_act[...] = wgu_ref[0].astype(act)
        wd_act[...] = wd_ref[0].astype(act)

    _wait_rows(gather(None, slot), MOE_BLOCK)

    @pl.when(i >= 2)
    def _():
        _wait_rows(scatter(None, slot), MOE_BLOCK)

    def step(compute, scatter_prev):
        x = _slabs_to_rows(xbuf.at[slot]).astype(act) if compute else None
        _start_rows(gather(tok_next_ref, other), MOE_BLOCK)
        if scatter_prev:
            _start_rows(scatter(dst_prev_ref, other), MOE_BLOCK)
        if compute:
            gu = _dot(x, wgu_act[...]) + bgu_ref[0]
            gate = jnp.minimum(gu[:, :D_FF], SWIGLU_LIMIT)
            up = jnp.clip(gu[:, D_FF:], -SWIGLU_LIMIT, SWIGLU_LIMIT)
            a = (up + 1.0) * (gate * _sigmoid(SWIGLU_ALPHA * gate))
            y = _dot(a.astype(act), wd_act[...]) + bd_ref[0]
            for c in range(ROW_TILES):
                ybuf[slot, :, c, :] = y[:, c * LANES:(c + 1) * LANES]
        else:
            ybuf[slot] = jnp.zeros((MOE_BLOCK, ROW_TILES, LANES), F32)

    pl.when((i < n_used) & (i > 0))(lambda: step(True, True))
    pl.when((i < n_used) & (i == 0))(lambda: step(True, False))
    pl.when(i >= n_used)(lambda: step(False, True))

    @pl.when(i == last)
    def _():
        _start_rows(scatter(dst_ref, slot), MOE_BLOCK)
        _wait_rows(scatter(None, other), MOE_BLOCK)
        _wait_rows(scatter(None, slot), MOE_BLOCK)
        _wait_rows(gather(None, other), MOE_BLOCK)


def _experts(block_expert, n_used, slot_tok, slot_dst, hn, w_gate_up, b_gate_up, w_down, b_down, *, y4_rows):
    n_blocks = block_expert.shape[0]
    assert n_blocks >= 2
    tok3 = slot_tok.reshape(n_blocks, 1, MOE_BLOCK)
    dst3 = slot_dst.reshape(n_blocks, 1, MOE_BLOCK)
    smem_blk = lambda f: pl.BlockSpec((1, 1, MOE_BLOCK), f, memory_space=pltpu.SMEM)
    grid_spec = pltpu.PrefetchScalarGridSpec(
        num_scalar_prefetch=2,
        grid=(n_blocks,),
        in_specs=[
            smem_blk(lambda i, be, nu: (i, 0, 0)),
            smem_blk(lambda i, be, nu: (jnp.minimum(i + 1, n_blocks - 1), 0, 0)),
            smem_blk(lambda i, be, nu: (jnp.maximum(i - 1, 0), 0, 0)),
            smem_blk(lambda i, be, nu: (i, 0, 0)),
            pl.BlockSpec(memory_space=pl.ANY),
            pl.BlockSpec((1, D_MODEL, 2 * D_FF), lambda i, be, nu: (be[i], 0, 0)),
            pl.BlockSpec((1, 1, 2 * D_FF), lambda i, be, nu: (be[i], 0, 0)),
            pl.BlockSpec((1, D_FF, D_MODEL), lambda i, be, nu: (be[i], 0, 0)),
            pl.BlockSpec((1, 1, D_MODEL), lambda i, be, nu: (be[i], 0, 0)),
        ],
        out_specs=pl.BlockSpec(memory_space=pl.ANY),
        scratch_shapes=[
            pltpu.VMEM((2, MOE_BLOCK, ROW_TILES, LANES), F32),
            pltpu.VMEM((2, MOE_BLOCK, ROW_TILES, LANES), F32),
            pltpu.VMEM((D_MODEL, 2 * D_FF), _MXU_DTYPE),
            pltpu.VMEM((D_FF, D_MODEL), _MXU_DTYPE),
            pltpu.SemaphoreType.DMA((2,)),
            pltpu.SemaphoreType.DMA((2,)),
        ],
    )
    return pl.pallas_call(
        _experts_kernel,
        out_shape=jax.ShapeDtypeStruct((y4_rows, ROW_TILES, LANES), F32),
        grid_spec=grid_spec,
        compiler_params=pltpu.CompilerParams(
            dimension_semantics=("arbitrary",), vmem_limit_bytes=VMEM_LIMIT),
        name="experts",
    )(block_expert, n_used, tok3, tok3, dst3, dst3, hn, w_gate_up,
      b_gate_up.reshape(N_EXPERTS, 1, 2 * D_FF), w_down, b_down.reshape(N_EXPERTS, 1, D_MODEL))


def _combine_kernel(h_ref, prob_ref, g_ref, *refs):
    y_refs, o_ref = refs[:TOP_K], refs[TOP_K]
    prob = prob_ref[...]
    h = h_ref[...]
    for k in range(TOP_K):
        h = h + prob[:, k:k + 1] * _slabs_to_rows(y_refs[k])
    o_ref[...] = (h * lax.rsqrt(jnp.mean(h * h, axis=-1, keepdims=True) + EPS)) * g_ref[...]


def _combine(y4, h, prob, g, *, plane, tm):
    n = h.shape[0]
    assert plane % tm == 0
    row_tile = lambda i: (i, 0)
    plane_tile = lambda k: (lambda i: (k * (plane // tm) + i, 0, 0))
    return pl.pallas_call(
        _combine_kernel,
        out_shape=jax.ShapeDtypeStruct((n, D_MODEL), F32),
        grid=(n // tm,),
        in_specs=[
            pl.BlockSpec((tm, D_MODEL), row_tile),
            pl.BlockSpec((tm, LANES), row_tile),
            pl.BlockSpec((1, D_MODEL), lambda i: (0, 0)),
        ] + [pl.BlockSpec((tm, ROW_TILES, LANES), plane_tile(k)) for k in range(TOP_K)],
        out_specs=pl.BlockSpec((tm, D_MODEL), row_tile),
        compiler_params=pltpu.CompilerParams(
            dimension_semantics=("arbitrary",), vmem_limit_bytes=VMEM_LIMIT),
        name="combine",
    )(h, prob, g, *([y4] * TOP_K))


def _rope_tables(seq):
    inv_freq = ROPE_THETA ** (-jnp.arange(0, HALF_DIM, dtype=F32) / HALF_DIM)
    ang = jnp.arange(seq, dtype=jnp.int32).astype(F32)[:, None] * inv_freq[None, :]
    cos, sin = jnp.cos(ang), jnp.sin(ang)
    reps = LANES // HEAD_DIM
    cos_t = jnp.tile(jnp.concatenate([cos, cos], axis=1), (1, reps))
    sin_t = jnp.tile(jnp.concatenate([-sin, sin], axis=1), (1, reps))
    return cos_t, sin_t


def _tile_rows(seq, want):
    tm = min(want, seq)
    assert seq % tm == 0 and tm % MOBA_BLOCK == 0, (seq, tm)
    return tm


def kernel(x, norm_mix_g, w_in, conv_w, w_conv_out, w_attn_out, w_out, norm_ffn_g, router_w, router_b,
           w_gate_up, b_gate_up, w_down, b_down, norm_final_g):
    batch, seq, d = x.shape
    assert d == D_MODEL and seq % MOBA_BLOCK == 0
    n = batch * seq
    n_kb = seq // MOBA_BLOCK
    act = _MXU_DTYPE
    x2 = x.reshape(n, d)
    tm_proj = _tile_rows(seq, 512)
    tm_mix = _tile_rows(seq, 256)
    tm_out = _tile_rows(seq, 256)

    cos_t, sin_t = _rope_tables(seq)
    zc, qt, k, vt, gates, kmean = _inproj(
        x2, norm_mix_g.reshape(1, d), w_in.astype(act), cos_t, sin_t, seq=seq, tm=tm_proj)
    attn = _moba(qt, k, vt, kmean.reshape(batch, n_kb, d), batch=batch, n_kb=n_kb)

    router_w_pad = jnp.zeros((d, LANES), F32).at[:, :N_EXPERTS].set(router_w).astype(act)
    router_b_pad = jnp.full((1, LANES), MASKED, F32).at[0, :N_EXPERTS].set(router_b)
    tri = (jnp.arange(tm_mix)[:, None] > jnp.arange(tm_mix)[None, :]).astype(act)
    h, hn, route, prob, counts = _mix(
        zc, gates, attn, x2, conv_w, w_conv_out.astype(act), w_attn_out.astype(act), w_out.astype(act),
        norm_ffn_g.reshape(1, d), router_w_pad, router_b_pad, tri, seq=seq, tm=tm_mix)

    nk = n * TOP_K
    n_slots = ((nk + N_EXPERTS * (MOE_BLOCK - 1) + MOE_BLOCK - 1) // MOE_BLOCK) * MOE_BLOCK
    n_blocks = n_slots // MOE_BLOCK
    cnt = counts[0, :N_EXPERTS].astype(jnp.int32)
    padded = ((cnt + MOE_BLOCK - 1) // MOE_BLOCK) * MOE_BLOCK
    padded_end = jnp.cumsum(padded)
    start_pad = padded_end - padded
    expert = route[:, :TOP_K]
    rank = route[:, TOP_K:2 * TOP_K]
    dest = (start_pad[expert] + rank).reshape(nk)
    n_pad = n_slots - nk
    plane = n + n_pad // TOP_K
    slot_pair = jnp.full((n_slots,), -1, jnp.int32).at[dest].set(jnp.arange(nk, dtype=jnp.int32))
    routed = slot_pair >= 0
    pad_rank = jnp.cumsum(jnp.where(routed, 0, 1).astype(jnp.int32)) - 1
    slot_tok = jnp.where(routed, slot_pair // TOP_K, 0)
    slot_dst = jnp.where(routed, (slot_pair % TOP_K) * plane + slot_pair // TOP_K,
                         (pad_rank % TOP_K) * plane + n + pad_rank // TOP_K)
    block_start = jnp.arange(n_blocks, dtype=jnp.int32) * MOE_BLOCK
    block_expert = jnp.minimum(
        jnp.sum((padded_end[None, :] <= block_start[:, None]).astype(jnp.int32), axis=1), N_EXPERTS - 1)
    n_used = (padded_end[-1:] // MOE_BLOCK).astype(jnp.int32)

    y4 = _experts(block_expert, n_used, slot_tok, slot_dst, hn, w_gate_up, b_gate_up, w_down, b_down,
                  y4_rows=TOP_K * plane)
    out = _combine(y4, h, prob, norm_final_g.reshape(1, d), plane=plane, tm=tm_out)
    return out.reshape(batch, seq, d)
```

```python
import functools

import jax
import jax.numpy as jnp
from jax import lax
from jax.experimental import pallas as pl
from jax.experimental.pallas import tpu as pltpu

D_MODEL = 1024
N_HEADS = 16
HEAD_DIM = 64
HALF_DIM = HEAD_DIM // 2
CONV_WIDTH = 3
MOBA_BLOCK = 256
TOP_BLOCKS = 3
ROPE_THETA = 10000.0
N_EXPERTS = 32
TOP_K = 4
D_FF = D_MODEL
SWIGLU_LIMIT = 7.0
SWIGLU_ALPHA = 1.702
MOE_BLOCK = 128
EPS = 1e-5
IN_PROJ_CHUNKS = 8

LANES = 128
SUBLANES = 8
HEADS_PER_GROUP = LANES // HEAD_DIM
N_HEAD_GROUPS = N_HEADS // HEADS_PER_GROUP
MOBA_GROUP = 8
LOG2_E = 1.4426950408889634
MAX_LAGGED_RISE = 60.0
MASKED = -1e30
VMEM_LIMIT = 56 * 1024 * 1024

_MXU_DTYPE = jnp.bfloat16
F32 = jnp.float32


def _dot(a, b):
    return jnp.dot(a, b, preferred_element_type=F32)


def _sigmoid(t):
    return 1.0 / (1.0 + jnp.exp(-t))


def _rope_cols(z, cos, sin_signed):
    lane = lax.broadcasted_iota(jnp.int32, (1, LANES), 1)
    first_half = (lane % HEAD_DIM) < HALF_DIM
    cols = []
    for g in range(D_MODEL // LANES):
        t = z[:, g * LANES:(g + 1) * LANES]
        from_above = pltpu.roll(t, LANES - HALF_DIM, axis=1)
        from_below = pltpu.roll(t, HALF_DIM, axis=1)
        cols.append(t * cos + jnp.where(first_half, from_above, from_below) * sin_signed)
    return jnp.concatenate(cols, axis=1)


def _inproj_kernel(x_ref, g_ref, w_ref, cos_ref, sin_ref,
                   zc_ref, qt_ref, k_ref, vt_ref, gates_ref, kmean_ref, xn_ref, *, tm):
    j = pl.program_id(1)
    kb_per_tile = tm // MOBA_BLOCK

    @pl.when(j == 0)
    def _():
        x = x_ref[...]
        ms = jnp.mean(x * x, axis=-1, keepdims=True)
        xn_ref[...] = ((x * lax.rsqrt(ms + EPS)) * g_ref[...]).astype(xn_ref.dtype)

    z = _dot(xn_ref[...], w_ref[...])

    @pl.when(j < 3)
    def _():
        zc_ref[...] = z

    @pl.when(j == 3)
    def _():
        qt = (_rope_cols(z, cos_ref[...], sin_ref[...]) * (LOG2_E * HEAD_DIM ** -0.5)).T
        for c in range(kb_per_tile):
            qt_ref[c] = qt[:, c * MOBA_BLOCK:(c + 1) * MOBA_BLOCK].astype(qt_ref.dtype)

    @pl.when(j == 4)
    def _():
        kr = _rope_cols(z, cos_ref[...], sin_ref[...])
        for c in range(kb_per_tile):
            blk = kr[c * MOBA_BLOCK:(c + 1) * MOBA_BLOCK, :]
            k_ref[c] = blk.astype(k_ref.dtype)
            kmean_ref[0, c:c + 1, :] = jnp.mean(blk, axis=0, keepdims=True)

    @pl.when(j == 5)
    def _():
        vt = z.T
        for c in range(kb_per_tile):
            vt_ref[c] = vt[:, c * MOBA_BLOCK:(c + 1) * MOBA_BLOCK].astype(vt_ref.dtype)

    @pl.when(j >= 6)
    def _():
        gates_ref[...] = z


def _inproj(x2, g, w_in, cos, sin_signed, *, seq, tm):
    n = x2.shape[0]
    n_tiles = n // tm
    tiles_per_seq = seq // tm
    kb_per_tile = tm // MOBA_BLOCK
    n_kb = n // MOBA_BLOCK
    act = _MXU_DTYPE
    out_shape = (
        jax.ShapeDtypeStruct((n, 3 * D_MODEL), F32),
        jax.ShapeDtypeStruct((n_kb, D_MODEL, MOBA_BLOCK), act),
        jax.ShapeDtypeStruct((n_kb, MOBA_BLOCK, D_MODEL), act),
        jax.ShapeDtypeStruct((n_kb, D_MODEL, MOBA_BLOCK), act),
        jax.ShapeDtypeStruct((n, 2 * D_MODEL), F32),
        jax.ShapeDtypeStruct((n_tiles, kb_per_tile, D_MODEL), F32),
    )
    row_tile = lambda i, j: (i, 0)
    blk3 = lambda i, j: (i, 0, 0)
    return pl.pallas_call(
        functools.partial(_inproj_kernel, tm=tm),
        out_shape=out_shape,
        grid=(n_tiles, IN_PROJ_CHUNKS),
        in_specs=[
            pl.BlockSpec((tm, D_MODEL), row_tile),
            pl.BlockSpec((1, D_MODEL), lambda i, j: (0, 0)),
            pl.BlockSpec((D_MODEL, D_MODEL), lambda i, j: (0, j)),
            pl.BlockSpec((tm, LANES), lambda i, j: (i % tiles_per_seq, 0)),
            pl.BlockSpec((tm, LANES), lambda i, j: (i % tiles_per_seq, 0)),
        ],
        out_specs=(
            pl.BlockSpec((tm, D_MODEL), lambda i, j: (i, jnp.minimum(j, 2))),
            pl.BlockSpec((kb_per_tile, D_MODEL, MOBA_BLOCK), blk3),
            pl.BlockSpec((kb_per_tile, MOBA_BLOCK, D_MODEL), blk3),
            pl.BlockSpec((kb_per_tile, D_MODEL, MOBA_BLOCK), blk3),
            pl.BlockSpec((tm, D_MODEL), lambda i, j: (i, jnp.clip(j - 6, 0, 1))),
            pl.BlockSpec((1, kb_per_tile, D_MODEL), blk3),
        ),
        scratch_shapes=[pltpu.VMEM((tm, D_MODEL), act)],
        compiler_params=pltpu.CompilerParams(
            dimension_semantics=("arbitrary", "arbitrary"), vmem_limit_bytes=VMEM_LIMIT),
        name="inproj",
    )(x2, g, w_in, cos, sin_signed)


def _moba_kernel(qt_ref, k_ref, vt_ref, kmean_ref, onehot_ref, o_ref, qaug_ref, s_ref, p_ref, *, n_kb, group):
    qb = pl.program_id(2)
    act = qt_ref.dtype
    blk = MOBA_BLOCK
    qt = qt_ref[0]
    km = kmean_ref[0].astype(act)
    dim_row = lax.broadcasted_iota(jnp.int32, (LANES, 1), 0)
    kb_row = lax.broadcasted_iota(jnp.int32, (n_kb, 1), 0)
    n_bias_rows = LANES

    q_aug = []
    for h in range(HEADS_PER_GROUP):
        in_head = (dim_row >= h * HEAD_DIM) & (dim_row < (h + 1) * HEAD_DIM)
        qh = jnp.where(in_head, qt, jnp.zeros_like(qt))
        gate = _dot(km, qh)
        work = jnp.where(kb_row < qb, gate, -jnp.inf)
        sel = jnp.zeros(gate.shape, jnp.bool_)
        for _ in range(TOP_BLOCKS):
            top = jnp.max(work, axis=0, keepdims=True)
            first = jnp.min(jnp.where(work == top, kb_row, n_kb), axis=0, keepdims=True)
            hit = kb_row == first
            sel = sel | (hit & (top > -jnp.inf))
            work = jnp.where(hit, -jnp.inf, work)
        bias = jnp.where(sel, 0.0, MASKED).astype(act)
        pad = jnp.zeros((n_bias_rows - n_kb, blk), act)
        q_aug.append(jnp.concatenate([qh, bias, pad], axis=0))

    for h in range(HEADS_PER_GROUP):
        qaug_ref[h] = q_aug[h]

    def v_rows(kb, h):
        return vt_ref[kb, h * HEAD_DIM:(h + 1) * HEAD_DIM, :]

    heads = range(HEADS_PER_GROUP)
    key_idx = lax.broadcasted_iota(jnp.int32, (blk, blk), 0)
    qry_idx = lax.broadcasted_iota(jnp.int32, (blk, blk), 1)
    causal = key_idx <= qry_idx
    k_own = jnp.concatenate([k_ref[qb], jnp.zeros((blk, LANES), act)], axis=1)
    ones_own = jnp.ones((2 * SUBLANES, blk), act)

    def own_scores(h):
        return jnp.where(causal, _dot(k_own, qaug_ref[h]), MASKED)

    def own_pv(h, p):
        return _dot(jnp.concatenate([v_rows(qb, h), ones_own], axis=0), p.astype(act))

    rows = group * blk
    n_groups = jnp.maximum((qb + group - 1) // group, 1)
    ones_rows = jnp.ones((2 * SUBLANES, rows), act)

    def k_operand(j):
        kb0 = pl.multiple_of(j * group, group)
        k_big = k_ref[pl.ds(kb0, group)].reshape(rows, LANES)
        pick = onehot_ref[pl.ds(kb0, group)].reshape(rows, LANES)
        return jnp.concatenate([k_big, pick], axis=1)

    def v_operand(j, h):
        kb0 = pl.multiple_of(j * group, group)
        v_big = jnp.concatenate([v_rows(kb0 + u, h) for u in range(group)], axis=1)
        return jnp.concatenate([v_big, ones_rows], axis=0)

    def write_out(accs):
        outs = [acc[:HEAD_DIM] / acc[HEAD_DIM:HEAD_DIM + 1] for acc in accs]
        o_ref[...] = jnp.concatenate(outs, axis=0).T.astype(o_ref.dtype)

    def score_stage(j, refs, slot):
        s = [_dot(k_operand(j), qaug_ref[h]) for h in heads]
        tops = []
        for h in heads:
            p_ref[slot, h] = jnp.exp2(s[h] - refs[h]).astype(act)
            tops.append(jnp.max(s[h], axis=0, keepdims=True))
        return tops

    def pv_stage(j, slot):
        return [_dot(v_operand(j, h), p_ref[slot, h]) for h in heads]

    def pipelined_body(j, carry):
        pv = pv_stage(j - 1, (j - 1) % 2)
        tops = score_stage(j, [carry[5 * h + 1] for h in heads], j % 2)
        new = []
        for h in heads:
            ref_prev, ref, acc, rise, peak = carry[5 * h:5 * h + 5]
            new += [ref, jnp.maximum(ref, tops[h]), (acc + pv[h]) * jnp.exp2(ref_prev - ref),
                    jnp.maximum(rise, tops[h] - ref), jnp.maximum(peak, tops[h])]
        return tuple(new)

    zero = jnp.zeros((1, blk), F32)
    tops = score_stage(0, [zero for _ in heads], 0)
    init = []
    for h in heads:
        init += [zero, jnp.maximum(zero, tops[h]), jnp.zeros((HEAD_DIM + 2 * SUBLANES, blk), F32), tops[h], tops[h]]
    fast = lax.fori_loop(1, n_groups, pipelined_body, tuple(init))
    pv = pv_stage(n_groups - 1, (n_groups - 1) % 2)
    accs, unsafe = [], []
    for h in heads:
        ref_prev, ref, acc, rise, peak = fast[5 * h:5 * h + 5]
        s = own_scores(h)
        top = jnp.max(s, axis=0, keepdims=True)
        accs.append((acc + pv[h]) * jnp.exp2(ref_prev - ref) + own_pv(h, jnp.exp2(s - ref)))
        unsafe.append(jnp.maximum(jnp.maximum(rise, top - ref), -jnp.maximum(peak, top)))
    write_out(accs)
    worst = jnp.max(functools.reduce(jnp.maximum, unsafe))

    @pl.when(worst > MAX_LAGGED_RISE)
    def _():
        own = []
        for h in heads:
            s = own_scores(h)
            m = jnp.max(s, axis=0, keepdims=True)
            own += [m, own_pv(h, jnp.exp2(s - m))]

        def exact_body(j, carry):
            k_aug = k_operand(j)
            new = []
            for h in heads:
                m, acc = carry[2 * h], carry[2 * h + 1]
                s_ref[h] = _dot(k_aug, qaug_ref[h])
                m_new = jnp.maximum(m, jnp.max(s_ref[h], axis=0, keepdims=True))
                p_ref[0, h] = jnp.exp2(s_ref[h] - m_new).astype(act)
                new += [m_new, jnp.exp2(m - m_new) * acc + _dot(v_operand(j, h), p_ref[0, h])]
            return tuple(new)

        safe = lax.fori_loop(0, n_groups, exact_body, tuple(own))
        write_out([safe[2 * h + 1] for h in heads])


def _moba(qt, k, vt, kmean, *, batch, n_kb):
    n = batch * n_kb * MOBA_BLOCK
    group = MOBA_GROUP if n_kb % MOBA_GROUP == 0 else 1
    onehot = jnp.arange(LANES)[None, None, :] == jnp.arange(n_kb)[:, None, None]
    onehot = jnp.broadcast_to(onehot, (n_kb, MOBA_BLOCK, LANES)).astype(qt.dtype)
    return pl.pallas_call(
        functools.partial(_moba_kernel, n_kb=n_kb, group=group),
        out_shape=jax.ShapeDtypeStruct((n, D_MODEL), qt.dtype),
        grid=(batch, N_HEAD_GROUPS, n_kb),
        in_specs=[
            pl.BlockSpec((1, LANES, MOBA_BLOCK), lambda b, g, q: (b * n_kb + q, g, 0)),
            pl.BlockSpec((n_kb, MOBA_BLOCK, LANES), lambda b, g, q: (b, 0, g)),
            pl.BlockSpec((n_kb, LANES, MOBA_BLOCK), lambda b, g, q: (b, g, 0)),
            pl.BlockSpec((1, n_kb, LANES), lambda b, g, q: (b, 0, g)),
            pl.BlockSpec((n_kb, MOBA_BLOCK, LANES), lambda b, g, q: (0, 0, 0)),
        ],
        out_specs=pl.BlockSpec((MOBA_BLOCK, LANES), lambda b, g, q: (b * n_kb + q, g)),
        scratch_shapes=[
            pltpu.VMEM((HEADS_PER_GROUP, 2 * LANES, MOBA_BLOCK), qt.dtype),
            pltpu.VMEM((HEADS_PER_GROUP, group * MOBA_BLOCK, MOBA_BLOCK), F32),
            pltpu.VMEM((2, HEADS_PER_GROUP, group * MOBA_BLOCK, MOBA_BLOCK), qt.dtype),
        ],
        compiler_params=pltpu.CompilerParams(
            dimension_semantics=("arbitrary", "arbitrary", "arbitrary"), vmem_limit_bytes=VMEM_LIMIT),
        name="moba",
    )(qt, k, vt, kmean, onehot)


def _mix_kernel(cb_ref, cc_ref, cx_ref, cch_ref, cxh_ref, gc_ref, ga_ref, attn_ref, x_ref,
                convw_ref, wco_ref, wao_ref, wo_ref, g_ref, rw_ref, rb_ref, tri_ref,
                h_ref, hn_ref, route_ref, prob_ref, counts_ref, carry_ref, *, tm, tiles_per_seq):
    i = pl.program_id(0)
    act = wco_ref.dtype

    @pl.when(i == 0)
    def _():
        carry_ref[...] = jnp.zeros_like(carry_ref)

    u = cc_ref[...] * cx_ref[...]
    keep = (i % tiles_per_seq != 0).astype(F32)
    halo = cch_ref[...] * cxh_ref[...] * keep
    prev1, prev2 = halo[SUBLANES - 1:SUBLANES, :], halo[SUBLANES - 2:SUBLANES - 1, :]
    row = lax.broadcasted_iota(jnp.int32, (tm, 1), 0)
    u1 = jnp.where(row == 0, prev1, pltpu.roll(u, 1, axis=0))
    u2 = jnp.where(row == 0, prev2, jnp.where(row == 1, prev1, pltpu.roll(u, 2, axis=0)))
    cw = convw_ref[...]
    conv = cw[0:1, :] * u2 + cw[1:2, :] * u1 + cw[2:3, :] * u
    y_conv = _dot((cb_ref[...] * conv).astype(act), wco_ref[...])
    y_attn = _dot(attn_ref[...], wao_ref[...])
    merged = _sigmoid(gc_ref[...]) * y_conv + _sigmoid(ga_ref[...]) * y_attn
    h = x_ref[...] + _dot(merged.astype(act), wo_ref[...])
    h_ref[...] = h
    hn = (h * lax.rsqrt(jnp.mean(h * h, axis=-1, keepdims=True) + EPS)) * g_ref[...]
    hn_ref[...] = hn

    logits = _dot(hn.astype(act), rw_ref[...]) + rb_ref[...]
    lane = lax.broadcasted_iota(jnp.int32, (1, LANES), 1)
    work = logits
    picks, vals = [], []
    sel = jnp.zeros(logits.shape, jnp.bool_)
    for _ in range(TOP_K):
        top = jnp.max(work, axis=-1, keepdims=True)
        first = jnp.min(jnp.where(work == top, lane, LANES), axis=-1, keepdims=True)
        hit = lane == first
        sel = sel | hit
        work = jnp.where(hit, -jnp.inf, work)
        picks.append(first)
        vals.append(top)
    exps = [jnp.exp(v - vals[0]) for v in vals]
    denom = exps[0] + exps[1] + exps[2] + exps[3]

    sel_f = jnp.where(sel, 1.0, 0.0)
    before = _dot(tri_ref[...], sel_f.astype(act)) + carry_ref[0:1, :]
    carry_ref[...] = carry_ref[...] + jnp.sum(sel_f, axis=0, keepdims=True)
    counts_ref[...] = carry_ref[...]

    route = jnp.zeros(logits.shape, jnp.int32)
    prob = jnp.zeros(logits.shape, F32)
    for t in range(TOP_K):
        rank = jnp.sum(jnp.where(lane == picks[t], before, 0.0), axis=-1, keepdims=True)
        route = jnp.where(lane == t, picks[t], route)
        route = jnp.where(lane == TOP_K + t, rank.astype(jnp.int32), route)
        prob = jnp.where(lane == t, exps[t] / denom, prob)
    route_ref[...] = route
    prob_ref[...] = prob


def _mix(zc, gates, attn, x2, conv_w, w_conv_out, w_attn_out, w_out, g, router_w, router_b, tri,
         *, seq, tm):
    n = x2.shape[0]
    n_tiles = n // tm
    halo_blocks = tm // SUBLANES
    row_tile = lambda i: (i, 0)
    const = lambda i: (0, 0)
    col = lambda c: (lambda i: (i, c))
    halo = lambda c: (lambda i: (jnp.maximum(i * halo_blocks - 1, 0), c))
    full = lambda a: pl.BlockSpec(a.shape, const)
    out_shape = (
        jax.ShapeDtypeStruct((n, D_MODEL), F32),
        jax.ShapeDtypeStruct((n, D_MODEL), F32),
        jax.ShapeDtypeStruct((n, LANES), jnp.int32),
        jax.ShapeDtypeStruct((n, LANES), F32),
        jax.ShapeDtypeStruct((SUBLANES, LANES), F32),
    )
    return pl.pallas_call(
        functools.partial(_mix_kernel, tm=tm, tiles_per_seq=seq // tm),
        out_shape=out_shape,
        grid=(n_tiles,),
        in_specs=[
            pl.BlockSpec((tm, D_MODEL), col(0)), pl.BlockSpec((tm, D_MODEL), col(1)),
            pl.BlockSpec((tm, D_MODEL), col(2)),
            pl.BlockSpec((SUBLANES, D_MODEL), halo(1)), pl.BlockSpec((SUBLANES, D_MODEL), halo(2)),
            pl.BlockSpec((tm, D_MODEL), col(0)), pl.BlockSpec((tm, D_MODEL), col(1)),
            pl.BlockSpec((tm, D_MODEL), row_tile), pl.BlockSpec((tm, D_MODEL), row_tile),
            full(conv_w), full(w_conv_out), full(w_attn_out), full(w_out), full(g),
            full(router_w), full(router_b), full(tri),
        ],
        out_specs=(
            pl.BlockSpec((tm, D_MODEL), row_tile), pl.BlockSpec((tm, D_MODEL), row_tile),
            pl.BlockSpec((tm, LANES), row_tile), pl.BlockSpec((tm, LANES), row_tile),
            pl.BlockSpec((SUBLANES, LANES), const),
        ),
        scratch_shapes=[pltpu.VMEM((SUBLANES, LANES), F32)],
        compiler_params=pltpu.CompilerParams(
            dimension_semantics=("arbitrary",), vmem_limit_bytes=VMEM_LIMIT),
        name="mix",
    )(zc, zc, zc, zc, zc, gates, gates, attn, x2, conv_w, w_conv_out, w_attn_out, w_out, g,
      router_w, router_b, tri)


def _row_copy(src, src_row, dst, dst_row, sem):
    return pltpu.make_async_copy(src.at[pl.ds(src_row, 1), :], dst.at[pl.ds(dst_row, 1), :], sem)


def _start_rows(copy_of_row, n_rows):
    for r in range(n_rows):
        copy_of_row(r).start()


def _wait_rows(copy_of_row, n_rows):
    def wait(r, c):
        copy_of_row(r).wait()
        return c
    lax.fori_loop(0, n_rows, wait, 0, unroll=8)


def _experts_kernel(be_ref, nused_ref, tok_ref, tok_next_ref, hn_hbm, wgu_ref, bgu_ref, wd_ref, bd_ref,
                    ys_ref, xbuf, wgu_act, wd_act, gsem):
    i = pl.program_id(0)
    last = pl.num_programs(0) - 1
    slot = i % 2
    other = 1 - slot
    act = wgu_act.dtype
    n_used = nused_ref[0]

    def gather(toks, s):
        return lambda r: _row_copy(hn_hbm, 0 if toks is None else toks[0, 0, r], xbuf.at[s], r, gsem.at[s])

    @pl.when(i == 0)
    def _():
        _start_rows(gather(tok_ref, 0), MOE_BLOCK)

    @pl.when((i == 0) | (be_ref[i] != be_ref[jnp.maximum(i - 1, 0)]))
    def _():
        wgu_act[...] = wgu_ref[0].astype(act)
        wd_act[...] = wd_ref[0].astype(act)

    _wait_rows(gather(None, slot), MOE_BLOCK)

    def step(compute):
        x = xbuf[slot].astype(act) if compute else None
        _start_rows(gather(tok_next_ref, other), MOE_BLOCK)
        if compute:
            gu = _dot(x, wgu_act[...]) + bgu_ref[0]
            gate = jnp.minimum(gu[:, :D_FF], SWIGLU_LIMIT)
            up = jnp.clip(gu[:, D_FF:], -SWIGLU_LIMIT, SWIGLU_LIMIT)
            a = (up + 1.0) * (gate * _sigmoid(SWIGLU_ALPHA * gate))
            ys_ref[...] = _dot(a.astype(act), wd_act[...]) + bd_ref[0]
        else:
            ys_ref[...] = jnp.zeros_like(ys_ref)

    pl.when(i < n_used)(lambda: step(True))
    pl.when(i >= n_used)(lambda: step(False))

    @pl.when(i == last)
    def _():
        _wait_rows(gather(None, other), MOE_BLOCK)


def _experts(block_expert, n_used, slot_tok, hn, w_gate_up, b_gate_up, w_down, b_down):
    n_blocks = block_expert.shape[0]
    tok3 = slot_tok.reshape(n_blocks, 1, MOE_BLOCK)
    smem_blk = lambda f: pl.BlockSpec((1, 1, MOE_BLOCK), f, memory_space=pltpu.SMEM)
    grid_spec = pltpu.PrefetchScalarGridSpec(
        num_scalar_prefetch=2,
        grid=(n_blocks,),
        in_specs=[
            smem_blk(lambda i, be, nu: (i, 0, 0)),
            smem_blk(lambda i, be, nu: (jnp.minimum(i + 1, n_blocks - 1), 0, 0)),
            pl.BlockSpec(memory_space=pl.ANY),
            pl.BlockSpec((1, D_MODEL, 2 * D_FF), lambda i, be, nu: (be[i], 0, 0)),
            pl.BlockSpec((1, 1, 2 * D_FF), lambda i, be, nu: (be[i], 0, 0)),
            pl.BlockSpec((1, D_FF, D_MODEL), lambda i, be, nu: (be[i], 0, 0)),
            pl.BlockSpec((1, 1, D_MODEL), lambda i, be, nu: (be[i], 0, 0)),
        ],
        out_specs=pl.BlockSpec((MOE_BLOCK, D_MODEL), lambda i, be, nu: (i, 0)),
        scratch_shapes=[
            pltpu.VMEM((2, MOE_BLOCK, D_MODEL), F32),
            pltpu.VMEM((D_MODEL, 2 * D_FF), _MXU_DTYPE),
            pltpu.VMEM((D_FF, D_MODEL), _MXU_DTYPE),
            pltpu.SemaphoreType.DMA((2,)),
        ],
    )
    return pl.pallas_call(
        _experts_kernel,
        out_shape=jax.ShapeDtypeStruct((n_blocks * MOE_BLOCK, D_MODEL), F32),
        grid_spec=grid_spec,
        compiler_params=pltpu.CompilerParams(
            dimension_semantics=("arbitrary",), vmem_limit_bytes=VMEM_LIMIT),
        name="experts",
    )(block_expert, n_used, tok3, tok3, hn, w_gate_up,
      b_gate_up.reshape(N_EXPERTS, 1, 2 * D_FF), w_down, b_down.reshape(N_EXPERTS, 1, D_MODEL))


def _combine_kernel(dest_ref, dest_next_ref, ys_hbm, h_ref, prob_ref, g_ref, o_ref, ybuf, sem, *, tm):
    i = pl.program_id(0)
    last = pl.num_programs(0) - 1
    slot = i % 2

    def start_gather(dests, s):
        def issue(r, c):
            for k in range(TOP_K):
                _row_copy(ys_hbm, dests[0, 0, r * TOP_K + k], ybuf.at[s, k], r, sem.at[s]).start()
            return c
        lax.fori_loop(0, tm, issue, 0, unroll=4)

    @pl.when(i == 0)
    def _():
        start_gather(dest_ref, 0)

    @pl.when(i < last)
    def _():
        start_gather(dest_next_ref, 1 - slot)

    for k in range(TOP_K):
        _wait_rows(lambda r: _row_copy(ys_hbm, 0, ybuf.at[slot, k], r, sem.at[slot]), tm)
    prob = prob_ref[...]
    h = h_ref[...]
    for k in range(TOP_K):
        h = h + prob[:, k:k + 1] * ybuf[slot, k]
    o_ref[...] = (h * lax.rsqrt(jnp.mean(h * h, axis=-1, keepdims=True) + EPS)) * g_ref[...]


def _combine(dest, ys, h, prob, g, *, tm):
    n = h.shape[0]
    n_tiles = n // tm
    dest3 = dest.reshape(n_tiles, 1, tm * TOP_K)
    smem_blk = lambda f: pl.BlockSpec((1, 1, tm * TOP_K), f, memory_space=pltpu.SMEM)
    row_tile = lambda i: (i, 0)
    return pl.pallas_call(
        functools.partial(_combine_kernel, tm=tm),
        out_shape=jax.ShapeDtypeStruct((n, D_MODEL), F32),
        grid=(n_tiles,),
        in_specs=[
            smem_blk(lambda i: (i, 0, 0)),
            smem_blk(lambda i: (jnp.minimum(i + 1, n_tiles - 1), 0, 0)),
            pl.BlockSpec(memory_space=pl.ANY),
            pl.BlockSpec((tm, D_MODEL), row_tile),
            pl.BlockSpec((tm, LANES), row_tile),
            pl.BlockSpec((1, D_MODEL), lambda i: (0, 0)),
        ],
        out_specs=pl.BlockSpec((tm, D_MODEL), row_tile),
        scratch_shapes=[
            pltpu.VMEM((2, TOP_K, tm, D_MODEL), F32),
            pltpu.SemaphoreType.DMA((2,)),
        ],
        compiler_params=pltpu.CompilerParams(
            dimension_semantics=("arbitrary",), vmem_limit_bytes=VMEM_LIMIT),
        name="combine",
    )(dest3, dest3, ys, h, prob, g)


def _rope_tables(seq):
    inv_freq = ROPE_THETA ** (-jnp.arange(0, HALF_DIM, dtype=F32) / HALF_DIM)
    ang = jnp.arange(seq, dtype=jnp.int32).astype(F32)[:, None] * inv_freq[None, :]
    cos, sin = jnp.cos(ang), jnp.sin(ang)
    reps = LANES // HEAD_DIM
    cos_t = jnp.tile(jnp.concatenate([cos, cos], axis=1), (1, reps))
    sin_t = jnp.tile(jnp.concatenate([-sin, sin], axis=1), (1, reps))
    return cos_t, sin_t


def _tile_rows(seq, want):
    tm = min(want, seq)
    assert seq % tm == 0 and tm % MOBA_BLOCK == 0, (seq, tm)
    return tm


def kernel(x, norm_mix_g, w_in, conv_w, w_conv_out, w_attn_out, w_out, norm_ffn_g, router_w, router_b,
           w_gate_up, b_gate_up, w_down, b_down, norm_final_g):
    batch, seq, d = x.shape
    assert d == D_MODEL and seq % MOBA_BLOCK == 0
    n = batch * seq
    n_kb = seq // MOBA_BLOCK
    act = _MXU_DTYPE
    x2 = x.reshape(n, d)
    tm_proj = _tile_rows(seq, 512)
    tm_mix = _tile_rows(seq, 256)
    tm_out = _tile_rows(seq, 256)

    cos_t, sin_t = _rope_tables(seq)
    zc, qt, k, vt, gates, kmean = _inproj(
        x2, norm_mix_g.reshape(1, d), w_in.astype(act), cos_t, sin_t, seq=seq, tm=tm_proj)
    attn = _moba(qt, k, vt, kmean.reshape(batch, n_kb, d), batch=batch, n_kb=n_kb)

    router_w_pad = jnp.zeros((d, LANES), F32).at[:, :N_EXPERTS].set(router_w).astype(act)
    router_b_pad = jnp.full((1, LANES), MASKED, F32).at[0, :N_EXPERTS].set(router_b)
    tri = (jnp.arange(tm_mix)[:, None] > jnp.arange(tm_mix)[None, :]).astype(act)
    h, hn, route, prob, counts = _mix(
        zc, gates, attn, x2, conv_w, w_conv_out.astype(act), w_attn_out.astype(act), w_out.astype(act),
        norm_ffn_g.reshape(1, d), router_w_pad, router_b_pad, tri, seq=seq, tm=tm_mix)

    nk = n * TOP_K
    n_slots = ((nk + N_EXPERTS * (MOE_BLOCK - 1) + MOE_BLOCK - 1) // MOE_BLOCK) * MOE_BLOCK
    n_blocks = n_slots // MOE_BLOCK
    cnt = counts[0, :N_EXPERTS].astype(jnp.int32)
    padded = ((cnt + MOE_BLOCK - 1) // MOE_BLOCK) * MOE_BLOCK
    padded_end = jnp.cumsum(padded)
    start_pad = padded_end - padded
    expert = route[:, :TOP_K]
    rank = route[:, TOP_K:2 * TOP_K]
    dest = (start_pad[expert] + rank).reshape(nk)
    slot_pair = jnp.full((n_slots,), -1, jnp.int32).at[dest].set(
        jnp.arange(nk, dtype=jnp.int32), unique_indices=True, mode="promise_in_bounds")
    slot_tok = jnp.maximum(slot_pair, 0) // TOP_K
    block_start = jnp.arange(n_blocks, dtype=jnp.int32) * MOE_BLOCK
    block_expert = jnp.minimum(
        jnp.sum((padded_end[None, :] <= block_start[:, None]).astype(jnp.int32), axis=1), N_EXPERTS - 1)
    n_used = (padded_end[-1:] // MOE_BLOCK).astype(jnp.int32)

    ys = _experts(block_expert, n_used, slot_tok, hn, w_gate_up, b_gate_up, w_down, b_down)
    out = _combine(dest, ys, h, prob, norm_final_g.reshape(1, d), tm=tm_out)
    return out.reshape(batch, seq, d)
```

```python
import functools

import jax
import jax.numpy as jnp
from jax import lax
from jax.experimental import pallas as pl
from jax.experimental.pallas import tpu as pltpu

D_MODEL = 1024
N_HEADS = 16
HEAD_DIM = 64
HALF_DIM = HEAD_DIM // 2
CONV_WIDTH = 3
MOBA_BLOCK = 256
TOP_BLOCKS = 3
ROPE_THETA = 10000.0
N_EXPERTS = 32
TOP_K = 4
D_FF = D_MODEL
SWIGLU_LIMIT = 7.0
SWIGLU_ALPHA = 1.702
EXPERT_ROWS = 512
EPS = 1e-5
IN_PROJ_CHUNKS = 8

LANES = 128
SUBLANES = 8
HEADS_PER_GROUP = LANES // HEAD_DIM
N_HEAD_GROUPS = N_HEADS // HEADS_PER_GROUP
MOBA_GROUP = 8
LOG2_E = 1.4426950408889634
MAX_LAGGED_RISE = 60.0
MASKED = -1e30
VMEM_LIMIT = 56 * 1024 * 1024

_MXU_DTYPE = jnp.bfloat16
F32 = jnp.float32


def _dot(a, b):
    return jnp.dot(a, b, preferred_element_type=F32)


def _sigmoid(t):
    return 1.0 / (1.0 + jnp.exp(-t))


def _rope_cols(z, cos, sin_signed):
    lane = lax.broadcasted_iota(jnp.int32, (1, LANES), 1)
    first_half = (lane % HEAD_DIM) < HALF_DIM
    cols = []
    for g in range(D_MODEL // LANES):
        t = z[:, g * LANES:(g + 1) * LANES]
        from_above = pltpu.roll(t, LANES - HALF_DIM, axis=1)
        from_below = pltpu.roll(t, HALF_DIM, axis=1)
        cols.append(t * cos + jnp.where(first_half, from_above, from_below) * sin_signed)
    return jnp.concatenate(cols, axis=1)


def _inproj_kernel(x_ref, g_ref, w_ref, cos_ref, sin_ref,
                   zc_ref, qt_ref, k_ref, vt_ref, gates_ref, kmean_ref, xn_ref, *, tm):
    j = pl.program_id(1)
    kb_per_tile = tm // MOBA_BLOCK

    @pl.when(j == 0)
    def _():
        x = x_ref[...]
        ms = jnp.mean(x * x, axis=-1, keepdims=True)
        xn_ref[...] = ((x * lax.rsqrt(ms + EPS)) * g_ref[...]).astype(xn_ref.dtype)

    z = _dot(xn_ref[...], w_ref[...])

    @pl.when(j < 3)
    def _():
        zc_ref[...] = z

    @pl.when(j == 3)
    def _():
        qt = (_rope_cols(z, cos_ref[...], sin_ref[...]) * (LOG2_E * HEAD_DIM ** -0.5)).T
        for c in range(kb_per_tile):
            qt_ref[c] = qt[:, c * MOBA_BLOCK:(c + 1) * MOBA_BLOCK].astype(qt_ref.dtype)

    @pl.when(j == 4)
    def _():
        kr = _rope_cols(z, cos_ref[...], sin_ref[...])
        for c in range(kb_per_tile):
            blk = kr[c * MOBA_BLOCK:(c + 1) * MOBA_BLOCK, :]
            k_ref[c] = blk.astype(k_ref.dtype)
            kmean_ref[0, c:c + 1, :] = jnp.mean(blk, axis=0, keepdims=True)

    @pl.when(j == 5)
    def _():
        vt = z.T
        for c in range(kb_per_tile):
            vt_ref[c] = vt[:, c * MOBA_BLOCK:(c + 1) * MOBA_BLOCK].astype(vt_ref.dtype)

    @pl.when(j >= 6)
    def _():
        gates_ref[...] = z


def _inproj(x2, g, w_in, cos, sin_signed, *, seq, tm):
    n = x2.shape[0]
    n_tiles = n // tm
    tiles_per_seq = seq // tm
    kb_per_tile = tm // MOBA_BLOCK
    n_kb = n // MOBA_BLOCK
    act = _MXU_DTYPE
    out_shape = (
        jax.ShapeDtypeStruct((n, 3 * D_MODEL), F32),
        jax.ShapeDtypeStruct((n_kb, D_MODEL, MOBA_BLOCK), act),
        jax.ShapeDtypeStruct((n_kb, MOBA_BLOCK, D_MODEL), act),
        jax.ShapeDtypeStruct((n_kb, D_MODEL, MOBA_BLOCK), act),
        jax.ShapeDtypeStruct((n, 2 * D_MODEL), F32),
        jax.ShapeDtypeStruct((n_tiles, kb_per_tile, D_MODEL), F32),
    )
    row_tile = lambda i, j: (i, 0)
    blk3 = lambda i, j: (i, 0, 0)
    return pl.pallas_call(
        functools.partial(_inproj_kernel, tm=tm),
        out_shape=out_shape,
        grid=(n_tiles, IN_PROJ_CHUNKS),
        in_specs=[
            pl.BlockSpec((tm, D_MODEL), row_tile),
            pl.BlockSpec((1, D_MODEL), lambda i, j: (0, 0)),
            pl.BlockSpec((D_MODEL, D_MODEL), lambda i, j: (0, j)),
            pl.BlockSpec((tm, LANES), lambda i, j: (i % tiles_per_seq, 0)),
            pl.BlockSpec((tm, LANES), lambda i, j: (i % tiles_per_seq, 0)),
        ],
        out_specs=(
            pl.BlockSpec((tm, D_MODEL), lambda i, j: (i, jnp.minimum(j, 2))),
            pl.BlockSpec((kb_per_tile, D_MODEL, MOBA_BLOCK), blk3),
            pl.BlockSpec((kb_per_tile, MOBA_BLOCK, D_MODEL), blk3),
            pl.BlockSpec((kb_per_tile, D_MODEL, MOBA_BLOCK), blk3),
            pl.BlockSpec((tm, D_MODEL), lambda i, j: (i, jnp.clip(j - 6, 0, 1))),
            pl.BlockSpec((1, kb_per_tile, D_MODEL), blk3),
        ),
        scratch_shapes=[pltpu.VMEM((tm, D_MODEL), act)],
        compiler_params=pltpu.CompilerParams(
            dimension_semantics=("arbitrary", "arbitrary"), vmem_limit_bytes=VMEM_LIMIT),
        name="inproj",
    )(x2, g, w_in, cos, sin_signed)


def _moba_kernel(qt_ref, k_ref, vt_ref, kmean_ref, onehot_ref, o_ref, qaug_ref, s_ref, p_ref, *, n_kb, group):
    qb = pl.program_id(2)
    act = qt_ref.dtype
    blk = MOBA_BLOCK
    qt = qt_ref[0]
    km = kmean_ref[0].astype(act)
    dim_row = lax.broadcasted_iota(jnp.int32, (LANES, 1), 0)
    kb_row = lax.broadcasted_iota(jnp.int32, (n_kb, 1), 0)
    n_bias_rows = LANES

    q_aug = []
    for h in range(HEADS_PER_GROUP):
        in_head = (dim_row >= h * HEAD_DIM) & (dim_row < (h + 1) * HEAD_DIM)
        qh = jnp.where(in_head, qt, jnp.zeros_like(qt))
        gate = _dot(km, qh)
        work = jnp.where(kb_row < qb, gate, -jnp.inf)
        sel = jnp.zeros(gate.shape, jnp.bool_)
        for _ in range(TOP_BLOCKS):
            top = jnp.max(work, axis=0, keepdims=True)
            first = jnp.min(jnp.where(work == top, kb_row, n_kb), axis=0, keepdims=True)
            hit = kb_row == first
            sel = sel | (hit & (top > -jnp.inf))
            work = jnp.where(hit, -jnp.inf, work)
        bias = jnp.where(sel, 0.0, MASKED).astype(act)
        pad = jnp.zeros((n_bias_rows - n_kb, blk), act)
        q_aug.append(jnp.concatenate([qh, bias, pad], axis=0))

    for h in range(HEADS_PER_GROUP):
        qaug_ref[h] = q_aug[h]

    def v_rows(kb, h):
        return vt_ref[kb, h * HEAD_DIM:(h + 1) * HEAD_DIM, :]

    heads = range(HEADS_PER_GROUP)
    key_idx = lax.broadcasted_iota(jnp.int32, (blk, blk), 0)
    qry_idx = lax.broadcasted_iota(jnp.int32, (blk, blk), 1)
    causal = key_idx <= qry_idx
    k_own = jnp.concatenate([k_ref[qb], jnp.zeros((blk, LANES), act)], axis=1)
    ones_own = jnp.ones((2 * SUBLANES, blk), act)

    def own_scores(h):
        return jnp.where(causal, _dot(k_own, qaug_ref[h]), MASKED)

    def own_pv(h, p):
        return _dot(jnp.concatenate([v_rows(qb, h), ones_own], axis=0), p.astype(act))

    rows = group * blk
    n_groups = jnp.maximum((qb + group - 1) // group, 1)
    ones_rows = jnp.ones((2 * SUBLANES, rows), act)

    def k_operand(j):
        kb0 = pl.multiple_of(j * group, group)
        k_big = k_ref[pl.ds(kb0, group)].reshape(rows, LANES)
        pick = onehot_ref[pl.ds(kb0, group)].reshape(rows, LANES)
        return jnp.concatenate([k_big, pick], axis=1)

    def v_operand(j, h):
        kb0 = pl.multiple_of(j * group, group)
        v_big = jnp.concatenate([v_rows(kb0 + u, h) for u in range(group)], axis=1)
        return jnp.concatenate([v_big, ones_rows], axis=0)

    def write_out(accs):
        outs = [acc[:HEAD_DIM] / acc[HEAD_DIM:HEAD_DIM + 1] for acc in accs]
        o_ref[...] = jnp.concatenate(outs, axis=0).T.astype(o_ref.dtype)

    def score_stage(j, refs, slot):
        s = [_dot(k_operand(j), qaug_ref[h]) for h in heads]
        tops = []
        for h in heads:
            p_ref[slot, h] = jnp.exp2(s[h] - refs[h]).astype(act)
            tops.append(jnp.max(s[h], axis=0, keepdims=True))
        return tops

    def pv_stage(j, slot):
        return [_dot(v_operand(j, h), p_ref[slot, h]) for h in heads]

    def pipelined_body(j, carry):
        pv = pv_stage(j - 1, (j - 1) % 2)
        tops = score_stage(j, [carry[5 * h + 1] for h in heads], j % 2)
        new = []
        for h in heads:
            ref_prev, ref, acc, rise, peak = carry[5 * h:5 * h + 5]
            new += [ref, jnp.maximum(ref, tops[h]), (acc + pv[h]) * jnp.exp2(ref_prev - ref),
                    jnp.maximum(rise, tops[h] - ref), jnp.maximum(peak, tops[h])]
        return tuple(new)

    zero = jnp.zeros((1, blk), F32)
    tops = score_stage(0, [zero for _ in heads], 0)
    init = []
    for h in heads:
        init += [zero, jnp.maximum(zero, tops[h]), jnp.zeros((HEAD_DIM + 2 * SUBLANES, blk), F32), tops[h], tops[h]]
    fast = lax.fori_loop(1, n_groups, pipelined_body, tuple(init))
    pv = pv_stage(n_groups - 1, (n_groups - 1) % 2)
    accs, unsafe = [], []
    for h in heads:
        ref_prev, ref, acc, rise, peak = fast[5 * h:5 * h + 5]
        s = own_scores(h)
        top = jnp.max(s, axis=0, keepdims=True)
        accs.append((acc + pv[h]) * jnp.exp2(ref_prev - ref) + own_pv(h, jnp.exp2(s - ref)))
        unsafe.append(jnp.maximum(jnp.maximum(rise, top - ref), -jnp.maximum(peak, top)))
    write_out(accs)
    worst = jnp.max(functools.reduce(jnp.maximum, unsafe))

    @pl.when(worst > MAX_LAGGED_RISE)
    def _():
        own = []
        for h in heads:
            s = own_scores(h)
            m = jnp.max(s, axis=0, keepdims=True)
            own += [m, own_pv(h, jnp.exp2(s - m))]

        def exact_body(j, carry):
            k_aug = k_operand(j)
            new = []
            for h in heads:
                m, acc = carry[2 * h], carry[2 * h + 1]
                s_ref[h] = _dot(k_aug, qaug_ref[h])
                m_new = jnp.maximum(m, jnp.max(s_ref[h], axis=0, keepdims=True))
                p_ref[0, h] = jnp.exp2(s_ref[h] - m_new).astype(act)
                new += [m_new, jnp.exp2(m - m_new) * acc + _dot(v_operand(j, h), p_ref[0, h])]
            return tuple(new)

        safe = lax.fori_loop(0, n_groups, exact_body, tuple(own))
        write_out([safe[2 * h + 1] for h in heads])


def _moba(qt, k, vt, kmean, *, batch, n_kb):
    n = batch * n_kb * MOBA_BLOCK
    group = MOBA_GROUP if n_kb % MOBA_GROUP == 0 else 1
    onehot = jnp.arange(LANES)[None, None, :] == jnp.arange(n_kb)[:, None, None]
    onehot = jnp.broadcast_to(onehot, (n_kb, MOBA_BLOCK, LANES)).astype(qt.dtype)
    return pl.pallas_call(
        functools.partial(_moba_kernel, n_kb=n_kb, group=group),
        out_shape=jax.ShapeDtypeStruct((n, D_MODEL), qt.dtype),
        grid=(batch, N_HEAD_GROUPS, n_kb),
        in_specs=[
            pl.BlockSpec((1, LANES, MOBA_BLOCK), lambda b, g, q: (b * n_kb + q, g, 0)),
            pl.BlockSpec((n_kb, MOBA_BLOCK, LANES), lambda b, g, q: (b, 0, g)),
            pl.BlockSpec((n_kb, LANES, MOBA_BLOCK), lambda b, g, q: (b, g, 0)),
            pl.BlockSpec((1, n_kb, LANES), lambda b, g, q: (b, 0, g)),
            pl.BlockSpec((n_kb, MOBA_BLOCK, LANES), lambda b, g, q: (0, 0, 0)),
        ],
        out_specs=pl.BlockSpec((MOBA_BLOCK, LANES), lambda b, g, q: (b * n_kb + q, g)),
        scratch_shapes=[
            pltpu.VMEM((HEADS_PER_GROUP, 2 * LANES, MOBA_BLOCK), qt.dtype),
            pltpu.VMEM((HEADS_PER_GROUP, group * MOBA_BLOCK, MOBA_BLOCK), F32),
            pltpu.VMEM((2, HEADS_PER_GROUP, group * MOBA_BLOCK, MOBA_BLOCK), qt.dtype),
        ],
        compiler_params=pltpu.CompilerParams(
            dimension_semantics=("arbitrary", "arbitrary", "arbitrary"), vmem_limit_bytes=VMEM_LIMIT),
        name="moba",
    )(qt, k, vt, kmean, onehot)


def _mix_kernel(cb_ref, cc_ref, cx_ref, cch_ref, cxh_ref, gc_ref, ga_ref, attn_ref, x_ref,
                convw_ref, wco_ref, wao_ref, wo_ref, g_ref, rw_ref, rb_ref, tri_ref,
                h_ref, hn_ref, route_ref, prob_ref, counts_ref, carry_ref, *, tm, tiles_per_seq):
    i = pl.program_id(0)
    act = wco_ref.dtype

    @pl.when(i == 0)
    def _():
        carry_ref[...] = jnp.zeros_like(carry_ref)

    u = cc_ref[...] * cx_ref[...]
    keep = (i % tiles_per_seq != 0).astype(F32)
    halo = cch_ref[...] * cxh_ref[...] * keep
    prev1, prev2 = halo[SUBLANES - 1:SUBLANES, :], halo[SUBLANES - 2:SUBLANES - 1, :]
    row = lax.broadcasted_iota(jnp.int32, (tm, 1), 0)
    u1 = jnp.where(row == 0, prev1, pltpu.roll(u, 1, axis=0))
    u2 = jnp.where(row == 0, prev2, jnp.where(row == 1, prev1, pltpu.roll(u, 2, axis=0)))
    cw = convw_ref[...]
    conv = cw[0:1, :] * u2 + cw[1:2, :] * u1 + cw[2:3, :] * u
    y_conv = _dot((cb_ref[...] * conv).astype(act), wco_ref[...])
    y_attn = _dot(attn_ref[...], wao_ref[...])
    merged = _sigmoid(gc_ref[...]) * y_conv + _sigmoid(ga_ref[...]) * y_attn
    h = x_ref[...] + _dot(merged.astype(act), wo_ref[...])
    h_ref[...] = h
    hn = (h * lax.rsqrt(jnp.mean(h * h, axis=-1, keepdims=True) + EPS)) * g_ref[...]
    hn_ref[...] = hn

    logits = _dot(hn.astype(act), rw_ref[...]) + rb_ref[...]
    lane = lax.broadcasted_iota(jnp.int32, (1, LANES), 1)
    work = logits
    picks, vals = [], []
    sel = jnp.zeros(logits.shape, jnp.bool_)
    for _ in range(TOP_K):
        top = jnp.max(work, axis=-1, keepdims=True)
        first = jnp.min(jnp.where(work == top, lane, LANES), axis=-1, keepdims=True)
        hit = lane == first
        sel = sel | hit
        work = jnp.where(hit, -jnp.inf, work)
        picks.append(first)
        vals.append(top)
    exps = [jnp.exp(v - vals[0]) for v in vals]
    denom = exps[0] + exps[1] + exps[2] + exps[3]

    sel_f = jnp.where(sel, 1.0, 0.0)
    before = _dot(tri_ref[...], sel_f.astype(act)) + carry_ref[0:1, :]
    carry_ref[...] = carry_ref[...] + jnp.sum(sel_f, axis=0, keepdims=True)
    counts_ref[...] = carry_ref[...]

    route = jnp.zeros(logits.shape, jnp.int32)
    prob = jnp.zeros(logits.shape, F32)
    for t in range(TOP_K):
        rank = jnp.sum(jnp.where(lane == picks[t], before, 0.0), axis=-1, keepdims=True)
        route = jnp.where(lane == t, picks[t], route)
        route = jnp.where(lane == TOP_K + t, rank.astype(jnp.int32), route)
        prob = jnp.where(lane == t, exps[t] / denom, prob)
    route_ref[...] = route
    prob_ref[...] = prob


def _mix(zc, gates, attn, x2, conv_w, w_conv_out, w_attn_out, w_out, g, router_w, router_b, tri,
         *, seq, tm):
    n = x2.shape[0]
    n_tiles = n // tm
    halo_blocks = tm // SUBLANES
    row_tile = lambda i: (i, 0)
    const = lambda i: (0, 0)
    col = lambda c: (lambda i: (i, c))
    halo = lambda c: (lambda i: (jnp.maximum(i * halo_blocks - 1, 0), c))
    full = lambda a: pl.BlockSpec(a.shape, const)
    out_shape = (
        jax.ShapeDtypeStruct((n, D_MODEL), F32),
        jax.ShapeDtypeStruct((n, D_MODEL), F32),
        jax.ShapeDtypeStruct((n, LANES), jnp.int32),
        jax.ShapeDtypeStruct((n, LANES), F32),
        jax.ShapeDtypeStruct((SUBLANES, LANES), F32),
    )
    return pl.pallas_call(
        functools.partial(_mix_kernel, tm=tm, tiles_per_seq=seq // tm),
        out_shape=out_shape,
        grid=(n_tiles,),
        in_specs=[
            pl.BlockSpec((tm, D_MODEL), col(0)), pl.BlockSpec((tm, D_MODEL), col(1)),
            pl.BlockSpec((tm, D_MODEL), col(2)),
            pl.BlockSpec((SUBLANES, D_MODEL), halo(1)), pl.BlockSpec((SUBLANES, D_MODEL), halo(2)),
            pl.BlockSpec((tm, D_MODEL), col(0)), pl.BlockSpec((tm, D_MODEL), col(1)),
            pl.BlockSpec((tm, D_MODEL), row_tile), pl.BlockSpec((tm, D_MODEL), row_tile),
            full(conv_w), full(w_conv_out), full(w_attn_out), full(w_out), full(g),
            full(router_w), full(router_b), full(tri),
        ],
        out_specs=(
            pl.BlockSpec((tm, D_MODEL), row_tile), pl.BlockSpec((tm, D_MODEL), row_tile),
            pl.BlockSpec((tm, LANES), row_tile), pl.BlockSpec((tm, LANES), row_tile),
            pl.BlockSpec((SUBLANES, LANES), const),
        ),
        scratch_shapes=[pltpu.VMEM((SUBLANES, LANES), F32)],
        compiler_params=pltpu.CompilerParams(
            dimension_semantics=("arbitrary",), vmem_limit_bytes=VMEM_LIMIT),
        name="mix",
    )(zc, zc, zc, zc, zc, gates, gates, attn, x2, conv_w, w_conv_out, w_attn_out, w_out, g,
      router_w, router_b, tri)


def _row_copy(src, src_row, dst, dst_row, sem):
    return pltpu.make_async_copy(src.at[pl.ds(src_row, 1), :], dst.at[pl.ds(dst_row, 1), :], sem)


def _start_rows(copy_of_row, n_rows):
    def issue(r, c):
        copy_of_row(r).start()
        return c
    lax.fori_loop(0, n_rows, issue, 0, unroll=8)


def _wait_rows(copy_of_row, n_rows):
    def wait(r, c):
        copy_of_row(r).wait()
        return c
    lax.fori_loop(0, n_rows, wait, 0, unroll=8)


def _experts_kernel(be_ref, nused_ref, tok_ref, tok_next_ref, hn_hbm, wgu_ref, bgu_ref, wd_ref, bd_ref,
                    ys_ref, xbuf, wgu_act, wd_act, gsem):
    i = pl.program_id(0)
    last = pl.num_programs(0) - 1
    slot = i % 2
    act = wgu_act.dtype

    def gather(toks, s):
        return lambda r: _row_copy(hn_hbm, 0 if toks is None else toks[0, 0, r], xbuf.at[s], r, gsem.at[s])

    @pl.when(i == 0)
    def _():
        _start_rows(gather(tok_ref, 0), EXPERT_ROWS)

    @pl.when(i < last)
    def _():
        _start_rows(gather(tok_next_ref, 1 - slot), EXPERT_ROWS)

    @pl.when((i == 0) | (be_ref[i] != be_ref[jnp.maximum(i - 1, 0)]))
    def _():
        wgu_act[...] = wgu_ref[0].astype(act)
        wd_act[...] = wd_ref[0].astype(act)

    _wait_rows(gather(None, slot), EXPERT_ROWS)

    @pl.when(i < nused_ref[0])
    def _():
        gu = _dot(xbuf[slot].astype(act), wgu_act[...]) + bgu_ref[0]
        gate = jnp.minimum(gu[:, :D_FF], SWIGLU_LIMIT)
        up = jnp.clip(gu[:, D_FF:], -SWIGLU_LIMIT, SWIGLU_LIMIT)
        a = (up + 1.0) * (gate * _sigmoid(SWIGLU_ALPHA * gate))
        ys_ref[...] = _dot(a.astype(act), wd_act[...]) + bd_ref[0]

    @pl.when(i >= nused_ref[0])
    def _():
        ys_ref[...] = jnp.zeros_like(ys_ref)


def _experts(block_expert, n_used, slot_tok, hn, w_gate_up, b_gate_up, w_down, b_down):
    n_blocks = block_expert.shape[0]
    tok3 = slot_tok.reshape(n_blocks, 1, EXPERT_ROWS)
    smem_blk = lambda f: pl.BlockSpec((1, 1, EXPERT_ROWS), f, memory_space=pltpu.SMEM)
    grid_spec = pltpu.PrefetchScalarGridSpec(
        num_scalar_prefetch=2,
        grid=(n_blocks,),
        in_specs=[
            smem_blk(lambda i, be, nu: (i, 0, 0)),
            smem_blk(lambda i, be, nu: (jnp.minimum(i + 1, n_blocks - 1), 0, 0)),
            pl.BlockSpec(memory_space=pl.ANY),
            pl.BlockSpec((1, D_MODEL, 2 * D_FF), lambda i, be, nu: (be[i], 0, 0)),
            pl.BlockSpec((1, 1, 2 * D_FF), lambda i, be, nu: (be[i], 0, 0)),
            pl.BlockSpec((1, D_FF, D_MODEL), lambda i, be, nu: (be[i], 0, 0)),
            pl.BlockSpec((1, 1, D_MODEL), lambda i, be, nu: (be[i], 0, 0)),
        ],
        out_specs=pl.BlockSpec((EXPERT_ROWS, D_MODEL), lambda i, be, nu: (i, 0)),
        scratch_shapes=[
            pltpu.VMEM((2, EXPERT_ROWS, D_MODEL), F32),
            pltpu.VMEM((D_MODEL, 2 * D_FF), _MXU_DTYPE),
            pltpu.VMEM((D_FF, D_MODEL), _MXU_DTYPE),
            pltpu.SemaphoreType.DMA((2,)),
        ],
    )
    return pl.pallas_call(
        _experts_kernel,
        out_shape=jax.ShapeDtypeStruct((n_blocks * EXPERT_ROWS, D_MODEL), F32),
        grid_spec=grid_spec,
        compiler_params=pltpu.CompilerParams(
            dimension_semantics=("arbitrary",), vmem_limit_bytes=VMEM_LIMIT),
        name="experts",
    )(block_expert, n_used, tok3, tok3, hn, w_gate_up,
      b_gate_up.reshape(N_EXPERTS, 1, 2 * D_FF), w_down, b_down.reshape(N_EXPERTS, 1, D_MODEL))


def _combine_kernel(dest_ref, dest_next_ref, ys_hbm, h_ref, prob_ref, g_ref, o_ref, ybuf, sem, *, tm):
    i = pl.program_id(0)
    last = pl.num_programs(0) - 1
    slot = i % 2

    def start_gather(dests, s):
        def issue(r, c):
            for k in range(TOP_K):
                _row_copy(ys_hbm, dests[0, 0, r * TOP_K + k], ybuf.at[s, k], r, sem.at[s]).start()
            return c
        lax.fori_loop(0, tm, issue, 0, unroll=4)

    @pl.when(i == 0)
    def _():
        start_gather(dest_ref, 0)

    @pl.when(i < last)
    def _():
        start_gather(dest_next_ref, 1 - slot)

    for k in range(TOP_K):
        _wait_rows(lambda r: _row_copy(ys_hbm, 0, ybuf.at[slot, k], r, sem.at[slot]), tm)
    prob = prob_ref[...]
    h = h_ref[...]
    for k in range(TOP_K):
        h = h + prob[:, k:k + 1] * ybuf[slot, k]
    o_ref[...] = (h * lax.rsqrt(jnp.mean(h * h, axis=-1, keepdims=True) + EPS)) * g_ref[...]


def _combine(dest, ys, h, prob, g, *, tm):
    n = h.shape[0]
    n_tiles = n // tm
    dest3 = dest.reshape(n_tiles, 1, tm * TOP_K)
    smem_blk = lambda f: pl.BlockSpec((1, 1, tm * TOP_K), f, memory_space=pltpu.SMEM)
    row_tile = lambda i: (i, 0)
    return pl.pallas_call(
        functools.partial(_combine_kernel, tm=tm),
        out_shape=jax.ShapeDtypeStruct((n, D_MODEL), F32),
        grid=(n_tiles,),
        in_specs=[
            smem_blk(lambda i: (i, 0, 0)),
            smem_blk(lambda i: (jnp.minimum(i + 1, n_tiles - 1), 0, 0)),
            pl.BlockSpec(memory_space=pl.ANY),
            pl.BlockSpec((tm, D_MODEL), row_tile),
            pl.BlockSpec((tm, LANES), row_tile),
            pl.BlockSpec((1, D_MODEL), lambda i: (0, 0)),
        ],
        out_specs=pl.BlockSpec((tm, D_MODEL), row_tile),
        scratch_shapes=[
            pltpu.VMEM((2, TOP_K, tm, D_MODEL), F32),
            pltpu.SemaphoreType.DMA((2,)),
        ],
        compiler_params=pltpu.CompilerParams(
            dimension_semantics=("arbitrary",), vmem_limit_bytes=VMEM_LIMIT),
        name="combine",
    )(dest3, dest3, ys, h, prob, g)


def _rope_tables(seq):
    inv_freq = ROPE_THETA ** (-jnp.arange(0, HALF_DIM, dtype=F32) / HALF_DIM)
    ang = jnp.arange(seq, dtype=jnp.int32).astype(F32)[:, None] * inv_freq[None, :]
    cos, sin = jnp.cos(ang), jnp.sin(ang)
    reps = LANES // HEAD_DIM
    cos_t = jnp.tile(jnp.concatenate([cos, cos], axis=1), (1, reps))
    sin_t = jnp.tile(jnp.concatenate([-sin, sin], axis=1), (1, reps))
    return cos_t, sin_t


def _tile_rows(seq, want):
    tm = min(want, seq)
    assert seq % tm == 0 and tm % MOBA_BLOCK == 0, (seq, tm)
    return tm


def kernel(x, norm_mix_g, w_in, conv_w, w_conv_out, w_attn_out, w_out, norm_ffn_g, router_w, router_b,
           w_gate_up, b_gate_up, w_down, b_down, norm_final_g):
    batch, seq, d = x.shape
    assert d == D_MODEL and seq % MOBA_BLOCK == 0
    n = batch * seq
    n_kb = seq // MOBA_BLOCK
    act = _MXU_DTYPE
    x2 = x.reshape(n, d)
    tm_proj = _tile_rows(seq, 512)
    tm_mix = _tile_rows(seq, 256)
    tm_out = _tile_rows(seq, 256)

    cos_t, sin_t = _rope_tables(seq)
    zc, qt, k, vt, gates, kmean = _inproj(
        x2, norm_mix_g.reshape(1, d), w_in.astype(act), cos_t, sin_t, seq=seq, tm=tm_proj)
    attn = _moba(qt, k, vt, kmean.reshape(batch, n_kb, d), batch=batch, n_kb=n_kb)

    router_w_pad = jnp.zeros((d, LANES), F32).at[:, :N_EXPERTS].set(router_w).astype(act)
    router_b_pad = jnp.full((1, LANES), MASKED, F32).at[0, :N_EXPERTS].set(router_b)
    tri = (jnp.arange(tm_mix)[:, None] > jnp.arange(tm_mix)[None, :]).astype(act)
    h, hn, route, prob, counts = _mix(
        zc, gates, attn, x2, conv_w, w_conv_out.astype(act), w_attn_out.astype(act), w_out.astype(act),
        norm_ffn_g.reshape(1, d), router_w_pad, router_b_pad, tri, seq=seq, tm=tm_mix)

    nk = n * TOP_K
    n_slots = ((nk + N_EXPERTS * (EXPERT_ROWS - 1) + EXPERT_ROWS - 1) // EXPERT_ROWS) * EXPERT_ROWS
    n_blocks = n_slots // EXPERT_ROWS
    cnt = counts[0, :N_EXPERTS].astype(jnp.int32)
    padded = ((cnt + EXPERT_ROWS - 1) // EXPERT_ROWS) * EXPERT_ROWS
    padded_end = jnp.cumsum(padded)
    start_pad = padded_end - padded
    expert = route[:, :TOP_K]
    rank = route[:, TOP_K:2 * TOP_K]
    dest = (start_pad[expert] + rank).reshape(nk)
    slot_pair = jnp.full((n_slots,), -1, jnp.int32).at[dest].set(
        jnp.arange(nk, dtype=jnp.int32), unique_indices=True, mode="promise_in_bounds")
    slot_tok = jnp.maximum(slot_pair, 0) // TOP_K
    block_start = jnp.arange(n_blocks, dtype=jnp.int32) * EXPERT_ROWS
    block_expert = jnp.minimum(
        jnp.sum((padded_end[None, :] <= block_start[:, None]).astype(jnp.int32), axis=1), N_EXPERTS - 1)
    n_used = (padded_end[-1:] // EXPERT_ROWS).astype(jnp.int32)

    ys = _experts(block_expert, n_used, slot_tok, hn, w_gate_up, b_gate_up, w_down, b_down)
    out = _combine(dest, ys, h, prob, norm_final_g.reshape(1, d), tm=tm_out)
    return out.reshape(batch, seq, d)
```

```python
import functools

import jax
import jax.numpy as jnp
from jax import lax
from jax.experimental import pallas as pl
from jax.experimental.pallas import tpu as pltpu

D_MODEL = 1024
N_HEADS = 16
HEAD_DIM = 64
HALF_DIM = HEAD_DIM // 2
CONV_WIDTH = 3
MOBA_BLOCK = 256
TOP_BLOCKS = 3
ROPE_THETA = 10000.0
N_EXPERTS = 32
TOP_K = 4
D_FF = D_MODEL
SWIGLU_LIMIT = 7.0
SWIGLU_ALPHA = 1.702
EXPERT_ROWS = 512
EPS = 1e-5
IN_PROJ_CHUNKS = 8

LANES = 128
SUBLANES = 8
HEADS_PER_GROUP = LANES // HEAD_DIM
N_HEAD_GROUPS = N_HEADS // HEADS_PER_GROUP
MOBA_GROUP = 8
MOBA_QBLOCKS = 2
LOG2_E = 1.4426950408889634
MAX_LAGGED_RISE = 60.0
MASKED = -1e30
VMEM_LIMIT = 56 * 1024 * 1024

_MXU_DTYPE = jnp.bfloat16
F32 = jnp.float32


def _dot(a, b):
    return jnp.dot(a, b, preferred_element_type=F32)


def _sigmoid(t):
    return 1.0 / (1.0 + jnp.exp(-t))


def _rope_cols(z, cos, sin_signed):
    lane = lax.broadcasted_iota(jnp.int32, (1, LANES), 1)
    first_half = (lane % HEAD_DIM) < HALF_DIM
    cols = []
    for g in range(D_MODEL // LANES):
        t = z[:, g * LANES:(g + 1) * LANES]
        from_above = pltpu.roll(t, LANES - HALF_DIM, axis=1)
        from_below = pltpu.roll(t, HALF_DIM, axis=1)
        cols.append(t * cos + jnp.where(first_half, from_above, from_below) * sin_signed)
    return jnp.concatenate(cols, axis=1)


def _inproj_kernel(x_ref, g_ref, w_ref, cos_ref, sin_ref,
                   zc_ref, qt_ref, k_ref, vt_ref, gates_ref, kmean_ref, xn_ref, *, tm):
    j = pl.program_id(1)
    kb_per_tile = tm // MOBA_BLOCK

    @pl.when(j == 0)
    def _():
        x = x_ref[...]
        ms = jnp.mean(x * x, axis=-1, keepdims=True)
        xn_ref[...] = ((x * lax.rsqrt(ms + EPS)) * g_ref[...]).astype(xn_ref.dtype)

    z = _dot(xn_ref[...], w_ref[...])

    @pl.when(j < 3)
    def _():
        zc_ref[...] = z

    @pl.when(j == 3)
    def _():
        qt = (_rope_cols(z, cos_ref[...], sin_ref[...]) * (LOG2_E * HEAD_DIM ** -0.5)).T
        for c in range(kb_per_tile):
            qt_ref[c] = qt[:, c * MOBA_BLOCK:(c + 1) * MOBA_BLOCK].astype(qt_ref.dtype)

    @pl.when(j == 4)
    def _():
        kr = _rope_cols(z, cos_ref[...], sin_ref[...])
        for c in range(kb_per_tile):
            blk = kr[c * MOBA_BLOCK:(c + 1) * MOBA_BLOCK, :]
            k_ref[c] = blk.astype(k_ref.dtype)
            kmean_ref[0, c:c + 1, :] = jnp.mean(blk, axis=0, keepdims=True)

    @pl.when(j == 5)
    def _():
        vt = z.T
        for c in range(kb_per_tile):
            vt_ref[c] = vt[:, c * MOBA_BLOCK:(c + 1) * MOBA_BLOCK].astype(vt_ref.dtype)

    @pl.when(j >= 6)
    def _():
        gates_ref[...] = z


def _inproj(x2, g, w_in, cos, sin_signed, *, seq, tm):
    n = x2.shape[0]
    n_tiles = n // tm
    tiles_per_seq = seq // tm
    kb_per_tile = tm // MOBA_BLOCK
    n_kb = n // MOBA_BLOCK
    act = _MXU_DTYPE
    out_shape = (
        jax.ShapeDtypeStruct((n, 3 * D_MODEL), F32),
        jax.ShapeDtypeStruct((n_kb, D_MODEL, MOBA_BLOCK), act),
        jax.ShapeDtypeStruct((n_kb, MOBA_BLOCK, D_MODEL), act),
        jax.ShapeDtypeStruct((n_kb, D_MODEL, MOBA_BLOCK), act),
        jax.ShapeDtypeStruct((n, 2 * D_MODEL), F32),
        jax.ShapeDtypeStruct((n_tiles, kb_per_tile, D_MODEL), F32),
    )
    row_tile = lambda i, j: (i, 0)
    blk3 = lambda i, j: (i, 0, 0)
    return pl.pallas_call(
        functools.partial(_inproj_kernel, tm=tm),
        out_shape=out_shape,
        grid=(n_tiles, IN_PROJ_CHUNKS),
        in_specs=[
            pl.BlockSpec((tm, D_MODEL), row_tile),
            pl.BlockSpec((1, D_MODEL), lambda i, j: (0, 0)),
            pl.BlockSpec((D_MODEL, D_MODEL), lambda i, j: (0, j)),
            pl.BlockSpec((tm, LANES), lambda i, j: (i % tiles_per_seq, 0)),
            pl.BlockSpec((tm, LANES), lambda i, j: (i % tiles_per_seq, 0)),
        ],
        out_specs=(
            pl.BlockSpec((tm, D_MODEL), lambda i, j: (i, jnp.minimum(j, 2))),
            pl.BlockSpec((kb_per_tile, D_MODEL, MOBA_BLOCK), blk3),
            pl.BlockSpec((kb_per_tile, MOBA_BLOCK, D_MODEL), blk3),
            pl.BlockSpec((kb_per_tile, D_MODEL, MOBA_BLOCK), blk3),
            pl.BlockSpec((tm, D_MODEL), lambda i, j: (i, jnp.clip(j - 6, 0, 1))),
            pl.BlockSpec((1, kb_per_tile, D_MODEL), blk3),
        ),
        scratch_shapes=[pltpu.VMEM((tm, D_MODEL), act)],
        compiler_params=pltpu.CompilerParams(
            dimension_semantics=("arbitrary", "arbitrary"), vmem_limit_bytes=VMEM_LIMIT),
        name="inproj",
    )(x2, g, w_in, cos, sin_signed)


def _moba_kernel(qt_ref, k_ref, vt_ref, kmean_ref, onehot_ref, o_ref, qaug_ref, s_ref, p_ref,
                 *, n_kb, group, qpb):
    qb0 = pl.program_id(2) * qpb
    act = qt_ref.dtype
    blk = MOBA_BLOCK
    width = qpb * blk
    heads = range(HEADS_PER_GROUP)
    lanes_of = lambda c: slice(c * blk, (c + 1) * blk)
    qt = jnp.concatenate([qt_ref[c] for c in range(qpb)], axis=1)
    km = kmean_ref[0].astype(act)
    dim_row = lax.broadcasted_iota(jnp.int32, (LANES, 1), 0)
    kb_row = lax.broadcasted_iota(jnp.int32, (n_kb, 1), 0)
    lane_qb = qb0 + lax.broadcasted_iota(jnp.int32, (1, width), 1) // blk
    n_bias_rows = LANES

    for h in heads:
        in_head = (dim_row >= h * HEAD_DIM) & (dim_row < (h + 1) * HEAD_DIM)
        qh = jnp.where(in_head, qt, jnp.zeros_like(qt))
        gate = _dot(km, qh)
        work = jnp.where(kb_row < lane_qb, gate, -jnp.inf)
        sel = jnp.zeros(gate.shape, jnp.bool_)
        for _ in range(TOP_BLOCKS):
            top = jnp.max(work, axis=0, keepdims=True)
            first = jnp.min(jnp.where(work == top, kb_row, n_kb), axis=0, keepdims=True)
            hit = kb_row == first
            sel = sel | (hit & (top > -jnp.inf))
            work = jnp.where(hit, -jnp.inf, work)
        bias = jnp.where(sel, 0.0, MASKED).astype(act)
        pad = jnp.zeros((n_bias_rows - n_kb, width), act)
        qaug_ref[h] = jnp.concatenate([qh, bias, pad], axis=0)

    def v_rows(kb, h):
        return vt_ref[kb, h * HEAD_DIM:(h + 1) * HEAD_DIM, :]

    key_idx = lax.broadcasted_iota(jnp.int32, (blk, blk), 0)
    qry_idx = lax.broadcasted_iota(jnp.int32, (blk, blk), 1)
    causal = key_idx <= qry_idx
    ones_own = jnp.ones((2 * SUBLANES, blk), act)

    def own_scores(h, c):
        k_own = jnp.concatenate([k_ref[qb0 + c], jnp.zeros((blk, LANES), act)], axis=1)
        return jnp.where(causal, _dot(k_own, qaug_ref[h, :, lanes_of(c)]), MASKED)

    def own_pv(h, c, p):
        return _dot(jnp.concatenate([v_rows(qb0 + c, h), ones_own], axis=0), p.astype(act))

    rows = group * blk
    n_groups = jnp.maximum((qb0 + qpb - 1 + group - 1) // group, 1)
    ones_rows = jnp.ones((2 * SUBLANES, rows), act)

    def k_operand(j):
        kb0 = pl.multiple_of(j * group, group)
        k_big = k_ref[pl.ds(kb0, group)].reshape(rows, LANES)
        pick = onehot_ref[pl.ds(kb0, group)].reshape(rows, LANES)
        return jnp.concatenate([k_big, pick], axis=1)

    def v_operand(j, h):
        kb0 = pl.multiple_of(j * group, group)
        v_big = jnp.concatenate([v_rows(kb0 + u, h) for u in range(group)], axis=1)
        return jnp.concatenate([v_big, ones_rows], axis=0)

    def write_out(accs):
        outs = [acc[:HEAD_DIM] / acc[HEAD_DIM:HEAD_DIM + 1] for acc in accs]
        o_ref[...] = jnp.concatenate(outs, axis=0).T.astype(o_ref.dtype)

    def score_stage(j, refs, slot):
        s = [_dot(k_operand(j), qaug_ref[h]) for h in heads]
        tops = []
        for h in heads:
            p_ref[slot, h] = jnp.exp2(s[h] - refs[h]).astype(act)
            tops.append(jnp.max(s[h], axis=0, keepdims=True))
        return tops

    def pv_stage(j, slot):
        return [_dot(v_operand(j, h), p_ref[slot, h]) for h in heads]

    def pipelined_body(j, carry):
        pv = pv_stage(j - 1, (j - 1) % 2)
        tops = score_stage(j, [carry[5 * h + 1] for h in heads], j % 2)
        new = []
        for h in heads:
            ref_prev, ref, acc, rise, peak = carry[5 * h:5 * h + 5]
            new += [ref, jnp.maximum(ref, tops[h]), (acc + pv[h]) * jnp.exp2(ref_prev - ref),
                    jnp.maximum(rise, tops[h] - ref), jnp.maximum(peak, tops[h])]
        return tuple(new)

    zero = jnp.zeros((1, width), F32)
    tops = score_stage(0, [zero for _ in heads], 0)
    init = []
    for h in heads:
        init += [zero, jnp.maximum(zero, tops[h]), jnp.zeros((HEAD_DIM + 2 * SUBLANES, width), F32),
                 tops[h], tops[h]]
    fast = lax.fori_loop(1, n_groups, pipelined_body, tuple(init))
    pv = pv_stage(n_groups - 1, (n_groups - 1) % 2)
    accs, unsafe = [], []
    for h in heads:
        ref_prev, ref, acc, rise, peak = fast[5 * h:5 * h + 5]
        past = (acc + pv[h]) * jnp.exp2(ref_prev - ref)
        pieces = []
        for c in range(qpb):
            s = own_scores(h, c)
            top = jnp.max(s, axis=0, keepdims=True)
            ref_c = ref[:, lanes_of(c)]
            pieces.append(past[:, lanes_of(c)] + own_pv(h, c, jnp.exp2(s - ref_c)))
            unsafe.append(jnp.maximum(jnp.maximum(rise[:, lanes_of(c)], top - ref_c),
                                      -jnp.maximum(peak[:, lanes_of(c)], top)))
        accs.append(jnp.concatenate(pieces, axis=1))
    write_out(accs)
    worst = jnp.max(functools.reduce(jnp.maximum, unsafe))

    @pl.when(worst > MAX_LAGGED_RISE)
    def _():
        own = []
        for h in heads:
            ms, pvs = [], []
            for c in range(qpb):
                s = own_scores(h, c)
                m = jnp.max(s, axis=0, keepdims=True)
                ms.append(m)
                pvs.append(own_pv(h, c, jnp.exp2(s - m)))
            own += [jnp.concatenate(ms, axis=1), jnp.concatenate(pvs, axis=1)]

        def exact_body(j, carry):
            k_aug = k_operand(j)
            new = []
            for h in heads:
                m, acc = carry[2 * h], carry[2 * h + 1]
                s_ref[h] = _dot(k_aug, qaug_ref[h])
                m_new = jnp.maximum(m, jnp.max(s_ref[h], axis=0, keepdims=True))
                p_ref[0, h] = jnp.exp2(s_ref[h] - m_new).astype(act)
                new += [m_new, jnp.exp2(m - m_new) * acc + _dot(v_operand(j, h), p_ref[0, h])]
            return tuple(new)

        safe = lax.fori_loop(0, n_groups, exact_body, tuple(own))
        write_out([safe[2 * h + 1] for h in heads])


def _moba(qt, k, vt, kmean, *, batch, n_kb):
    n = batch * n_kb * MOBA_BLOCK
    group = MOBA_GROUP if n_kb % MOBA_GROUP == 0 else 1
    qpb = MOBA_QBLOCKS if n_kb % MOBA_QBLOCKS == 0 else 1
    width = qpb * MOBA_BLOCK
    steps = n_kb // qpb
    onehot = jnp.arange(LANES)[None, None, :] == jnp.arange(n_kb)[:, None, None]
    onehot = jnp.broadcast_to(onehot, (n_kb, MOBA_BLOCK, LANES)).astype(qt.dtype)
    return pl.pallas_call(
        functools.partial(_moba_kernel, n_kb=n_kb, group=group, qpb=qpb),
        out_shape=jax.ShapeDtypeStruct((n, D_MODEL), qt.dtype),
        grid=(batch, N_HEAD_GROUPS, steps),
        in_specs=[
            pl.BlockSpec((qpb, LANES, MOBA_BLOCK), lambda b, g, q: (b * steps + q, g, 0)),
            pl.BlockSpec((n_kb, MOBA_BLOCK, LANES), lambda b, g, q: (b, 0, g)),
            pl.BlockSpec((n_kb, LANES, MOBA_BLOCK), lambda b, g, q: (b, g, 0)),
            pl.BlockSpec((1, n_kb, LANES), lambda b, g, q: (b, 0, g)),
            pl.BlockSpec((n_kb, MOBA_BLOCK, LANES), lambda b, g, q: (0, 0, 0)),
        ],
        out_specs=pl.BlockSpec((width, LANES), lambda b, g, q: (b * steps + q, g)),
        scratch_shapes=[
            pltpu.VMEM((HEADS_PER_GROUP, 2 * LANES, width), qt.dtype),
            pltpu.VMEM((HEADS_PER_GROUP, group * MOBA_BLOCK, width), F32),
            pltpu.VMEM((2, HEADS_PER_GROUP, group * MOBA_BLOCK, width), qt.dtype),
        ],
        compiler_params=pltpu.CompilerParams(
            dimension_semantics=("arbitrary", "arbitrary", "arbitrary"), vmem_limit_bytes=VMEM_LIMIT),
        name="moba",
    )(qt, k, vt, kmean, onehot)


def _mix_kernel(cb_ref, cc_ref, cx_ref, cch_ref, cxh_ref, gc_ref, ga_ref, attn_ref, x_ref,
                convw_ref, wco_ref, wao_ref, wo_ref, g_ref, rw_ref, rb_ref, tri_ref,
                h_ref, hn_ref, route_ref, prob_ref, counts_ref, carry_ref, *, tm, tiles_per_seq):
    i = pl.program_id(0)
    act = wco_ref.dtype

    @pl.when(i == 0)
    def _():
        carry_ref[...] = jnp.zeros_like(carry_ref)

    u = cc_ref[...] * cx_ref[...]
    keep = (i % tiles_per_seq != 0).astype(F32)
    halo = cch_ref[...] * cxh_ref[...] * keep
    prev1, prev2 = halo[SUBLANES - 1:SUBLANES, :], halo[SUBLANES - 2:SUBLANES - 1, :]
    row = lax.broadcasted_iota(jnp.int32, (tm, 1), 0)
    u1 = jnp.where(row == 0, prev1, pltpu.roll(u, 1, axis=0))
    u2 = jnp.where(row == 0, prev2, jnp.where(row == 1, prev1, pltpu.roll(u, 2, axis=0)))
    cw = convw_ref[...]
    conv = cw[0:1, :] * u2 + cw[1:2, :] * u1 + cw[2:3, :] * u
    y_conv = _dot((cb_ref[...] * conv).astype(act), wco_ref[...])
    y_attn = _dot(attn_ref[...], wao_ref[...])
    merged = _sigmoid(gc_ref[...]) * y_conv + _sigmoid(ga_ref[...]) * y_attn
    h = x_ref[...] + _dot(merged.astype(act), wo_ref[...])
    h_ref[...] = h
    hn = (h * lax.rsqrt(jnp.mean(h * h, axis=-1, keepdims=True) + EPS)) * g_ref[...]
    hn_ref[...] = hn

    logits = _dot(hn.astype(act), rw_ref[...]) + rb_ref[...]
    lane = lax.broadcasted_iota(jnp.int32, (1, LANES), 1)
    work = logits
    picks, vals = [], []
    sel = jnp.zeros(logits.shape, jnp.bool_)
    for _ in range(TOP_K):
        top = jnp.max(work, axis=-1, keepdims=True)
        first = jnp.min(jnp.where(work == top, lane, LANES), axis=-1, keepdims=True)
        hit = lane == first
        sel = sel | hit
        work = jnp.where(hit, -jnp.inf, work)
        picks.append(first)
        vals.append(top)
    exps = [jnp.exp(v - vals[0]) for v in vals]
    denom = exps[0] + exps[1] + exps[2] + exps[3]

    sel_f = jnp.where(sel, 1.0, 0.0)
    before = _dot(tri_ref[...], sel_f.astype(act)) + carry_ref[0:1, :]
    carry_ref[...] = carry_ref[...] + jnp.sum(sel_f, axis=0, keepdims=True)
    counts_ref[...] = carry_ref[...]

    route = jnp.zeros(logits.shape, jnp.int32)
    prob = jnp.zeros(logits.shape, F32)
    for t in range(TOP_K):
        rank = jnp.sum(jnp.where(lane == picks[t], before, 0.0), axis=-1, keepdims=True)
        route = jnp.where(lane == t, picks[t], route)
        route = jnp.where(lane == TOP_K + t, rank.astype(jnp.int32), route)
        prob = jnp.where(lane == t, exps[t] / denom, prob)
    route_ref[...] = route
    prob_ref[...] = prob


def _mix(zc, gates, attn, x2, conv_w, w_conv_out, w_attn_out, w_out, g, router_w, router_b, tri,
         *, seq, tm):
    n = x2.shape[0]
    n_tiles = n // tm
    halo_blocks = tm // SUBLANES
    row_tile = lambda i: (i, 0)
    const = lambda i: (0, 0)
    col = lambda c: (lambda i: (i, c))
    halo = lambda c: (lambda i: (jnp.maximum(i * halo_blocks - 1, 0), c))
    full = lambda a: pl.BlockSpec(a.shape, const)
    out_shape = (
        jax.ShapeDtypeStruct((n, D_MODEL), F32),
        jax.ShapeDtypeStruct((n, D_MODEL), F32),
        jax.ShapeDtypeStruct((n, LANES), jnp.int32),
        jax.ShapeDtypeStruct((n, LANES), F32),
        jax.ShapeDtypeStruct((SUBLANES, LANES), F32),
    )
    return pl.pallas_call(
        functools.partial(_mix_kernel, tm=tm, tiles_per_seq=seq // tm),
        out_shape=out_shape,
        grid=(n_tiles,),
        in_specs=[
            pl.BlockSpec((tm, D_MODEL), col(0)), pl.BlockSpec((tm, D_MODEL), col(1)),
            pl.BlockSpec((tm, D_MODEL), col(2)),
            pl.BlockSpec((SUBLANES, D_MODEL), halo(1)), pl.BlockSpec((SUBLANES, D_MODEL), halo(2)),
            pl.BlockSpec((tm, D_MODEL), col(0)), pl.BlockSpec((tm, D_MODEL), col(1)),
            pl.BlockSpec((tm, D_MODEL), row_tile), pl.BlockSpec((tm, D_MODEL), row_tile),
            full(conv_w), full(w_conv_out), full(w_attn_out), full(w_out), full(g),
            full(router_w), full(router_b), full(tri),
        ],
        out_specs=(
            pl.BlockSpec((tm, D_MODEL), row_tile), pl.BlockSpec((tm, D_MODEL), row_tile),
            pl.BlockSpec((tm, LANES), row_tile), pl.BlockSpec((tm, LANES), row_tile),
            pl.BlockSpec((SUBLANES, LANES), const),
        ),
        scratch_shapes=[pltpu.VMEM((SUBLANES, LANES), F32)],
        compiler_params=pltpu.CompilerParams(
            dimension_semantics=("arbitrary",), vmem_limit_bytes=VMEM_LIMIT),
        name="mix",
    )(zc, zc, zc, zc, zc, gates, gates, attn, x2, conv_w, w_conv_out, w_attn_out, w_out, g,
      router_w, router_b, tri)


def _row_copy(src, src_row, dst, dst_row, sem):
    return pltpu.make_async_copy(src.at[pl.ds(src_row, 1), :], dst.at[pl.ds(dst_row, 1), :], sem)


def _start_rows(copy_of_row, n_rows):
    def issue(r, c):
        copy_of_row(r).start()
        return c
    lax.fori_loop(0, n_rows, issue, 0, unroll=8)


def _wait_rows(copy_of_row, n_rows):
    def wait(r, c):
        copy_of_row(r).wait()
        return c
    lax.fori_loop(0, n_rows, wait, 0, unroll=8)


def _experts_kernel(be_ref, nused_ref, tok_ref, tok_next_ref, hn_hbm, wgu_ref, bgu_ref, wd_ref, bd_ref,
                    ys_ref, xbuf, wgu_act, wd_act, gsem):
    i = pl.program_id(0)
    last = pl.num_programs(0) - 1
    slot = i % 2
    act = wgu_act.dtype

    def gather(toks, s):
        return lambda r: _row_copy(hn_hbm, 0 if toks is None else toks[0, 0, r], xbuf.at[s], r, gsem.at[s])

    @pl.when(i == 0)
    def _():
        _start_rows(gather(tok_ref, 0), EXPERT_ROWS)

    @pl.when(i < last)
    def _():
        _start_rows(gather(tok_next_ref, 1 - slot), EXPERT_ROWS)

    @pl.when((i == 0) | (be_ref[i] != be_ref[jnp.maximum(i - 1, 0)]))
    def _():
        wgu_act[...] = wgu_ref[0].astype(act)
        wd_act[...] = wd_ref[0].astype(act)

    _wait_rows(gather(None, slot), EXPERT_ROWS)

    @pl.when(i < nused_ref[0])
    def _():
        gu = _dot(xbuf[slot].astype(act), wgu_act[...]) + bgu_ref[0]
        gate = jnp.minimum(gu[:, :D_FF], SWIGLU_LIMIT)
        up = jnp.clip(gu[:, D_FF:], -SWIGLU_LIMIT, SWIGLU_LIMIT)
        a = (up + 1.0) * (gate * _sigmoid(SWIGLU_ALPHA * gate))
        ys_ref[...] = _dot(a.astype(act), wd_act[...]) + bd_ref[0]

    @pl.when(i >= nused_ref[0])
    def _():
        ys_ref[...] = jnp.zeros_like(ys_ref)


def _experts(block_expert, n_used, slot_tok, hn, w_gate_up, b_gate_up, w_down, b_down):
    n_blocks = block_expert.shape[0]
    tok3 = slot_tok.reshape(n_blocks, 1, EXPERT_ROWS)
    smem_blk = lambda f: pl.BlockSpec((1, 1, EXPERT_ROWS), f, memory_space=pltpu.SMEM)
    grid_spec = pltpu.PrefetchScalarGridSpec(
        num_scalar_prefetch=2,
        grid=(n_blocks,),
        in_specs=[
            smem_blk(lambda i, be, nu: (i, 0, 0)),
            smem_blk(lambda i, be, nu: (jnp.minimum(i + 1, n_blocks - 1), 0, 0)),
            pl.BlockSpec(memory_space=pl.ANY),
            pl.BlockSpec((1, D_MODEL, 2 * D_FF), lambda i, be, nu: (be[i], 0, 0)),
            pl.BlockSpec((1, 1, 2 * D_FF), lambda i, be, nu: (be[i], 0, 0)),
            pl.BlockSpec((1, D_FF, D_MODEL), lambda i, be, nu: (be[i], 0, 0)),
            pl.BlockSpec((1, 1, D_MODEL), lambda i, be, nu: (be[i], 0, 0)),
        ],
        out_specs=pl.BlockSpec((EXPERT_ROWS, D_MODEL), lambda i, be, nu: (i, 0)),
        scratch_shapes=[
            pltpu.VMEM((2, EXPERT_ROWS, D_MODEL), F32),
            pltpu.VMEM((D_MODEL, 2 * D_FF), _MXU_DTYPE),
            pltpu.VMEM((D_FF, D_MODEL), _MXU_DTYPE),
            pltpu.SemaphoreType.DMA((2,)),
        ],
    )
    return pl.pallas_call(
        _experts_kernel,
        out_shape=jax.ShapeDtypeStruct((n_blocks * EXPERT_ROWS, D_MODEL), F32),
        grid_spec=grid_spec,
        compiler_params=pltpu.CompilerParams(
            dimension_semantics=("arbitrary",), vmem_limit_bytes=VMEM_LIMIT),
        name="experts",
    )(block_expert, n_used, tok3, tok3, hn, w_gate_up,
      b_gate_up.reshape(N_EXPERTS, 1, 2 * D_FF), w_down, b_down.reshape(N_EXPERTS, 1, D_MODEL))


def _combine_kernel(dest_ref, dest_next_ref, ys_hbm, h_ref, prob_ref, g_ref, o_ref, ybuf, sem, *, tm):
    i = pl.program_id(0)
    last = pl.num_programs(0) - 1
    slot = i % 2

    def start_gather(dests, s):
        def issue(r, c):
            for k in range(TOP_K):
                _row_copy(ys_hbm, dests[0, 0, r * TOP_K + k], ybuf.at[s, k], r, sem.at[s]).start()
            return c
        lax.fori_loop(0, tm, issue, 0, unroll=4)

    @pl.when(i == 0)
    def _():
        start_gather(dest_ref, 0)

    @pl.when(i < last)
    def _():
        start_gather(dest_next_ref, 1 - slot)

    for k in range(TOP_K):
        _wait_rows(lambda r: _row_copy(ys_hbm, 0, ybuf.at[slot, k], r, sem.at[slot]), tm)
    prob = prob_ref[...]
    h = h_ref[...]
    for k in range(TOP_K):
        h = h + prob[:, k:k + 1] * ybuf[slot, k]
    o_ref[...] = (h * lax.rsqrt(jnp.mean(h * h, axis=-1, keepdims=True) + EPS)) * g_ref[...]


def _combine(dest, ys, h, prob, g, *, tm):
    n = h.shape[0]
    n_tiles = n // tm
    dest3 = dest.reshape(n_tiles, 1, tm * TOP_K)
    smem_blk = lambda f: pl.BlockSpec((1, 1, tm * TOP_K), f, memory_space=pltpu.SMEM)
    row_tile = lambda i: (i, 0)
    return pl.pallas_call(
        functools.partial(_combine_kernel, tm=tm),
        out_shape=jax.ShapeDtypeStruct((n, D_MODEL), F32),
        grid=(n_tiles,),
        in_specs=[
            smem_blk(lambda i: (i, 0, 0)),
            smem_blk(lambda i: (jnp.minimum(i + 1, n_tiles - 1), 0, 0)),
            pl.BlockSpec(memory_space=pl.ANY),
            pl.BlockSpec((tm, D_MODEL), row_tile),
            pl.BlockSpec((tm, LANES), row_tile),
            pl.BlockSpec((1, D_MODEL), lambda i: (0, 0)),
        ],
        out_specs=pl.BlockSpec((tm, D_MODEL), row_tile),
        scratch_shapes=[
            pltpu.VMEM((2, TOP_K, tm, D_MODEL), F32),
            pltpu.SemaphoreType.DMA((2,)),
        ],
        compiler_params=pltpu.CompilerParams(
            dimension_semantics=("arbitrary",), vmem_limit_bytes=VMEM_LIMIT),
        name="combine",
    )(dest3, dest3, ys, h, prob, g)


def _rope_tables(seq):
    inv_freq = ROPE_THETA ** (-jnp.arange(0, HALF_DIM, dtype=F32) / HALF_DIM)
    ang = jnp.arange(seq, dtype=jnp.int32).astype(F32)[:, None] * inv_freq[None, :]
    cos, sin = jnp.cos(ang), jnp.sin(ang)
    reps = LANES // HEAD_DIM
    cos_t = jnp.tile(jnp.concatenate([cos, cos], axis=1), (1, reps))
    sin_t = jnp.tile(jnp.concatenate([-sin, sin], axis=1), (1, reps))
    return cos_t, sin_t


def _tile_rows(seq, want):
    tm = min(want, seq)
    assert seq % tm == 0 and tm % MOBA_BLOCK == 0, (seq, tm)
    return tm


def kernel(x, norm_mix_g, w_in, conv_w, w_conv_out, w_attn_out, w_out, norm_ffn_g, router_w, router_b,
           w_gate_up, b_gate_up, w_down, b_down, norm_final_g):
    batch, seq, d = x.shape
    assert d == D_MODEL and seq % MOBA_BLOCK == 0
    n = batch * seq
    n_kb = seq // MOBA_BLOCK
    act = _MXU_DTYPE
    x2 = x.reshape(n, d)
    tm_proj = _tile_rows(seq, 512)
    tm_mix = _tile_rows(seq, 256)
    tm_out = _tile_rows(seq, 256)

    cos_t, sin_t = _rope_tables(seq)
    zc, qt, k, vt, gates, kmean = _inproj(
        x2, norm_mix_g.reshape(1, d), w_in.astype(act), cos_t, sin_t, seq=seq, tm=tm_proj)
    attn = _moba(qt, k, vt, kmean.reshape(batch, n_kb, d), batch=batch, n_kb=n_kb)

    router_w_pad = jnp.zeros((d, LANES), F32).at[:, :N_EXPERTS].set(router_w).astype(act)
    router_b_pad = jnp.full((1, LANES), MASKED, F32).at[0, :N_EXPERTS].set(router_b)
    tri = (jnp.arange(tm_mix)[:, None] > jnp.arange(tm_mix)[None, :]).astype(act)
    h, hn, route, prob, counts = _mix(
        zc, gates, attn, x2, conv_w, w_conv_out.astype(act), w_attn_out.astype(act), w_out.astype(act),
        norm_ffn_g.reshape(1, d), router_w_pad, router_b_pad, tri, seq=seq, tm=tm_mix)

    nk = n * TOP_K
    n_slots = ((nk + N_EXPERTS * (EXPERT_ROWS - 1) + EXPERT_ROWS - 1) // EXPERT_ROWS) * EXPERT_ROWS
    n_blocks = n_slots // EXPERT_ROWS
    cnt = counts[0, :N_EXPERTS].astype(jnp.int32)
    padded = ((cnt + EXPERT_ROWS - 1) // EXPERT_ROWS) * EXPERT_ROWS
    padded_end = jnp.cumsum(padded)
    start_pad = padded_end - padded
    expert = route[:, :TOP_K]
    rank = route[:, TOP_K:2 * TOP_K]
    dest = (start_pad[expert] + rank).reshape(nk)
    slot_pair = jnp.full((n_slots,), -1, jnp.int32).at[dest].set(
        jnp.arange(nk, dtype=jnp.int32), unique_indices=True, mode="promise_in_bounds")
    slot_tok = jnp.maximum(slot_pair, 0) // TOP_K
    block_start = jnp.arange(n_blocks, dtype=jnp.int32) * EXPERT_ROWS
    block_expert = jnp.minimum(
        jnp.sum((padded_end[None, :] <= block_start[:, None]).astype(jnp.int32), axis=1), N_EXPERTS - 1)
    n_used = (padded_end[-1:] // EXPERT_ROWS).astype(jnp.int32)

    ys = _experts(block_expert, n_used, slot_tok, hn, w_gate_up, b_gate_up, w_down, b_down)
    out = _combine(dest, ys, h, prob, norm_final_g.reshape(1, d), tm=tm_out)
    return out.reshape(batch, seq, d)
```

```python
import functools

import jax
import jax.numpy as jnp
from jax import lax
from jax.experimental import pallas as pl
from jax.experimental.pallas import tpu as pltpu

D_MODEL = 1024
N_HEADS = 16
HEAD_DIM = 64
HALF_DIM = HEAD_DIM // 2
CONV_WIDTH = 3
MOBA_BLOCK = 256
TOP_BLOCKS = 3
ROPE_THETA = 10000.0
N_EXPERTS = 32
TOP_K = 4
D_FF = D_MODEL
SWIGLU_LIMIT = 7.0
SWIGLU_ALPHA = 1.702
EXPERT_ROWS = 512
EPS = 1e-5
IN_PROJ_CHUNKS = 8

LANES = 128
SUBLANES = 8
HEADS_PER_GROUP = LANES // HEAD_DIM
N_HEAD_GROUPS = N_HEADS // HEADS_PER_GROUP
MOBA_GROUP = 8
MOBA_QBLOCKS = 2
LOG2_E = 1.4426950408889634
MAX_LAGGED_RISE = 60.0
MASKED = -1e30
VMEM_LIMIT = 56 * 1024 * 1024

_MXU_DTYPE = jnp.bfloat16
F32 = jnp.float32


def _dot(a, b):
    return jnp.dot(a, b, preferred_element_type=F32)


def _sigmoid(t):
    return 1.0 / (1.0 + jnp.exp(-t))


def _rope_cols(z, cos, sin_signed):
    lane = lax.broadcasted_iota(jnp.int32, (1, LANES), 1)
    first_half = (lane % HEAD_DIM) < HALF_DIM
    cols = []
    for g in range(D_MODEL // LANES):
        t = z[:, g * LANES:(g + 1) * LANES]
        from_above = pltpu.roll(t, LANES - HALF_DIM, axis=1)
        from_below = pltpu.roll(t, HALF_DIM, axis=1)
        cols.append(t * cos + jnp.where(first_half, from_above, from_below) * sin_signed)
    return jnp.concatenate(cols, axis=1)


def _inproj_kernel(x_ref, g_ref, w_ref, cos_ref, sin_ref,
                   zc_ref, qt_ref, k_ref, vt_ref, gates_ref, kmean_ref, xn_ref, *, tm):
    j = pl.program_id(1)
    kb_per_tile = tm // MOBA_BLOCK

    @pl.when(j == 0)
    def _():
        x = x_ref[...]
        ms = jnp.mean(x * x, axis=-1, keepdims=True)
        xn_ref[...] = ((x * lax.rsqrt(ms + EPS)) * g_ref[...]).astype(xn_ref.dtype)

    z = _dot(xn_ref[...], w_ref[...])

    @pl.when(j < 3)
    def _():
        zc_ref[...] = z

    @pl.when(j == 3)
    def _():
        qt = (_rope_cols(z, cos_ref[...], sin_ref[...]) * (LOG2_E * HEAD_DIM ** -0.5)).T
        for c in range(kb_per_tile):
            qt_ref[c] = qt[:, c * MOBA_BLOCK:(c + 1) * MOBA_BLOCK].astype(qt_ref.dtype)

    @pl.when(j == 4)
    def _():
        kr = _rope_cols(z, cos_ref[...], sin_ref[...])
        for c in range(kb_per_tile):
            blk = kr[c * MOBA_BLOCK:(c + 1) * MOBA_BLOCK, :]
            k_ref[c] = blk.astype(k_ref.dtype)
            kmean_ref[0, c:c + 1, :] = jnp.mean(blk, axis=0, keepdims=True)

    @pl.when(j == 5)
    def _():
        vt = z.T
        for c in range(kb_per_tile):
            vt_ref[c] = vt[:, c * MOBA_BLOCK:(c + 1) * MOBA_BLOCK].astype(vt_ref.dtype)

    @pl.when(j >= 6)
    def _():
        gates_ref[...] = z


def _inproj(x2, g, w_in, cos, sin_signed, *, seq, tm):
    n = x2.shape[0]
    n_tiles = n // tm
    tiles_per_seq = seq // tm
    kb_per_tile = tm // MOBA_BLOCK
    n_kb = n // MOBA_BLOCK
    act = _MXU_DTYPE
    out_shape = (
        jax.ShapeDtypeStruct((n, 3 * D_MODEL), F32),
        jax.ShapeDtypeStruct((n_kb, D_MODEL, MOBA_BLOCK), act),
        jax.ShapeDtypeStruct((n_kb, MOBA_BLOCK, D_MODEL), act),
        jax.ShapeDtypeStruct((n_kb, D_MODEL, MOBA_BLOCK), act),
        jax.ShapeDtypeStruct((n, 2 * D_MODEL), F32),
        jax.ShapeDtypeStruct((n_tiles, kb_per_tile, D_MODEL), F32),
    )
    row_tile = lambda i, j: (i, 0)
    blk3 = lambda i, j: (i, 0, 0)
    return pl.pallas_call(
        functools.partial(_inproj_kernel, tm=tm),
        out_shape=out_shape,
        grid=(n_tiles, IN_PROJ_CHUNKS),
        in_specs=[
            pl.BlockSpec((tm, D_MODEL), row_tile),
            pl.BlockSpec((1, D_MODEL), lambda i, j: (0, 0)),
            pl.BlockSpec((D_MODEL, D_MODEL), lambda i, j: (0, j)),
            pl.BlockSpec((tm, LANES), lambda i, j: (i % tiles_per_seq, 0)),
            pl.BlockSpec((tm, LANES), lambda i, j: (i % tiles_per_seq, 0)),
        ],
        out_specs=(
            pl.BlockSpec((tm, D_MODEL), lambda i, j: (i, jnp.minimum(j, 2))),
            pl.BlockSpec((kb_per_tile, D_MODEL, MOBA_BLOCK), blk3),
            pl.BlockSpec((kb_per_tile, MOBA_BLOCK, D_MODEL), blk3),
            pl.BlockSpec((kb_per_tile, D_MODEL, MOBA_BLOCK), blk3),
            pl.BlockSpec((tm, D_MODEL), lambda i, j: (i, jnp.clip(j - 6, 0, 1))),
            pl.BlockSpec((1, kb_per_tile, D_MODEL), blk3),
        ),
        scratch_shapes=[pltpu.VMEM((tm, D_MODEL), act)],
        compiler_params=pltpu.CompilerParams(
            dimension_semantics=("arbitrary", "arbitrary"), vmem_limit_bytes=VMEM_LIMIT),
        name="inproj",
    )(x2, g, w_in, cos, sin_signed)


def _moba_kernel(qt_ref, k_ref, vt_ref, kmean_ref, onehot_ref, o_ref, qaug_ref, s_ref, p_ref,
                 *, n_kb, group, qpb):
    qb0 = pl.program_id(2) * qpb
    act = qt_ref.dtype
    blk = MOBA_BLOCK
    width = qpb * blk
    heads = range(HEADS_PER_GROUP)
    lanes_of = lambda c: slice(c * blk, (c + 1) * blk)
    qt = jnp.concatenate([qt_ref[c] for c in range(qpb)], axis=1)
    km = kmean_ref[0].astype(act)
    dim_row = lax.broadcasted_iota(jnp.int32, (LANES, 1), 0)
    kb_row = lax.broadcasted_iota(jnp.int32, (n_kb, 1), 0)
    lane_qb = qb0 + lax.broadcasted_iota(jnp.int32, (1, width), 1) // blk
    n_bias_rows = LANES

    for h in heads:
        in_head = (dim_row >= h * HEAD_DIM) & (dim_row < (h + 1) * HEAD_DIM)
        qh = jnp.where(in_head, qt, jnp.zeros_like(qt))
        gate = _dot(km, qh)
        work = jnp.where(kb_row < lane_qb, gate, -jnp.inf)
        sel = jnp.zeros(gate.shape, jnp.bool_)
        for _ in range(TOP_BLOCKS):
            top = jnp.max(work, axis=0, keepdims=True)
            first = jnp.min(jnp.where(work == top, kb_row, n_kb), axis=0, keepdims=True)
            hit = kb_row == first
            sel = sel | (hit & (top > -jnp.inf))
            work = jnp.where(hit, -jnp.inf, work)
        bias = jnp.where(sel, 0.0, MASKED).astype(act)
        pad = jnp.zeros((n_bias_rows - n_kb, width), act)
        qaug_ref[h] = jnp.concatenate([qh, bias, pad], axis=0)

    def v_rows(kb, h):
        return vt_ref[kb, h * HEAD_DIM:(h + 1) * HEAD_DIM, :]

    key_idx = lax.broadcasted_iota(jnp.int32, (blk, blk), 0)
    qry_idx = lax.broadcasted_iota(jnp.int32, (blk, blk), 1)
    causal = key_idx <= qry_idx
    ones_own = jnp.ones((2 * SUBLANES, blk), act)

    def own_scores(h, c):
        k_own = jnp.concatenate([k_ref[qb0 + c], jnp.zeros((blk, LANES), act)], axis=1)
        return jnp.where(causal, _dot(k_own, qaug_ref[h, :, lanes_of(c)]), MASKED)

    def own_pv(h, c, p):
        return _dot(jnp.concatenate([v_rows(qb0 + c, h), ones_own], axis=0), p.astype(act))

    rows = group * blk
    n_groups = jnp.maximum((qb0 + qpb - 1 + group - 1) // group, 1)
    ones_rows = jnp.ones((2 * SUBLANES, rows), act)

    def k_operand(j):
        kb0 = pl.multiple_of(j * group, group)
        k_big = k_ref[pl.ds(kb0, group)].reshape(rows, LANES)
        pick = onehot_ref[pl.ds(kb0, group)].reshape(rows, LANES)
        return jnp.concatenate([k_big, pick], axis=1)

    def v_operand(j, h):
        kb0 = pl.multiple_of(j * group, group)
        v_big = jnp.concatenate([v_rows(kb0 + u, h) for u in range(group)], axis=1)
        return jnp.concatenate([v_big, ones_rows], axis=0)

    def write_out(accs):
        outs = [acc[:HEAD_DIM] / acc[HEAD_DIM:HEAD_DIM + 1] for acc in accs]
        o_ref[...] = jnp.concatenate(outs, axis=0).T.astype(o_ref.dtype)

    def score_stage(j, refs, slot):
        s = [_dot(k_operand(j), qaug_ref[h]) for h in heads]
        tops = []
        for h in heads:
            p_ref[slot, h] = jnp.exp2(s[h] - refs[h]).astype(act)
            tops.append(jnp.max(s[h], axis=0, keepdims=True))
        return tops

    def pv_stage(j, slot):
        return [_dot(v_operand(j, h), p_ref[slot, h]) for h in heads]

    def pipelined_body(j, carry):
        pv = pv_stage(j - 1, (j - 1) % 2)
        tops = score_stage(j, [carry[5 * h + 1] for h in heads], j % 2)
        new = []
        for h in heads:
            ref_prev, ref, acc, rise, peak = carry[5 * h:5 * h + 5]
            new += [ref, jnp.maximum(ref, tops[h]), (acc + pv[h]) * jnp.exp2(ref_prev - ref),
                    jnp.maximum(rise, tops[h] - ref), jnp.maximum(peak, tops[h])]
        return tuple(new)

    zero = jnp.zeros((1, width), F32)
    tops = score_stage(0, [zero for _ in heads], 0)
    init = []
    for h in heads:
        init += [zero, jnp.maximum(zero, tops[h]), jnp.zeros((HEAD_DIM + 2 * SUBLANES, width), F32),
                 tops[h], tops[h]]
    fast = lax.fori_loop(1, n_groups, pipelined_body, tuple(init))
    pv = pv_stage(n_groups - 1, (n_groups - 1) % 2)
    accs, unsafe = [], []
    for h in heads:
        ref_prev, ref, acc, rise, peak = fast[5 * h:5 * h + 5]
        past = (acc + pv[h]) * jnp.exp2(ref_prev - ref)
        pieces = []
        for c in range(qpb):
            s = own_scores(h, c)
            top = jnp.max(s, axis=0, keepdims=True)
            ref_c = ref[:, lanes_of(c)]
            pieces.append(past[:, lanes_of(c)] + own_pv(h, c, jnp.exp2(s - ref_c)))
            unsafe.append(jnp.maximum(jnp.maximum(rise[:, lanes_of(c)], top - ref_c),
                                      -jnp.maximum(peak[:, lanes_of(c)], top)))
        accs.append(jnp.concatenate(pieces, axis=1))
    write_out(accs)
    worst = jnp.max(functools.reduce(jnp.maximum, unsafe))

    @pl.when(worst > MAX_LAGGED_RISE)
    def _():
        own = []
        for h in heads:
            ms, pvs = [], []
            for c in range(qpb):
                s = own_scores(h, c)
                m = jnp.max(s, axis=0, keepdims=True)
                ms.append(m)
                pvs.append(own_pv(h, c, jnp.exp2(s - m)))
            own += [jnp.concatenate(ms, axis=1), jnp.concatenate(pvs, axis=1)]

        def exact_body(j, carry):
            k_aug = k_operand(j)
            new = []
            for h in heads:
                m, acc = carry[2 * h], carry[2 * h + 1]
                s_ref[h] = _dot(k_aug, qaug_ref[h])
                m_new = jnp.maximum(m, jnp.max(s_ref[h], axis=0, keepdims=True))
                p_ref[0, h] = jnp.exp2(s_ref[h] - m_new).astype(act)
                new += [m_new, jnp.exp2(m - m_new) * acc + _dot(v_operand(j, h), p_ref[0, h])]
            return tuple(new)

        safe = lax.fori_loop(0, n_groups, exact_body, tuple(own))
        write_out([safe[2 * h + 1] for h in heads])


def _moba(qt, k, vt, kmean, *, batch, n_kb):
    n = batch * n_kb * MOBA_BLOCK
    group = MOBA_GROUP if n_kb % MOBA_GROUP == 0 else 1
    qpb = MOBA_QBLOCKS if n_kb % MOBA_QBLOCKS == 0 else 1
    width = qpb * MOBA_BLOCK
    steps = n_kb // qpb
    onehot = jnp.arange(LANES)[None, None, :] == jnp.arange(n_kb)[:, None, None]
    onehot = jnp.broadcast_to(onehot, (n_kb, MOBA_BLOCK, LANES)).astype(qt.dtype)
    return pl.pallas_call(
        functools.partial(_moba_kernel, n_kb=n_kb, group=group, qpb=qpb),
        out_shape=jax.ShapeDtypeStruct((n, D_MODEL), qt.dtype),
        grid=(batch, N_HEAD_GROUPS, steps),
        in_specs=[
            pl.BlockSpec((qpb, LANES, MOBA_BLOCK), lambda b, g, q: (b * steps + q, g, 0)),
            pl.BlockSpec((n_kb, MOBA_BLOCK, LANES), lambda b, g, q: (b, 0, g)),
            pl.BlockSpec((n_kb, LANES, MOBA_BLOCK), lambda b, g, q: (b, g, 0)),
            pl.BlockSpec((1, n_kb, LANES), lambda b, g, q: (b, 0, g)),
            pl.BlockSpec((n_kb, MOBA_BLOCK, LANES), lambda b, g, q: (0, 0, 0)),
        ],
        out_specs=pl.BlockSpec((width, LANES), lambda b, g, q: (b * steps + q, g)),
        scratch_shapes=[
            pltpu.VMEM((HEADS_PER_GROUP, 2 * LANES, width), qt.dtype),
            pltpu.VMEM((HEADS_PER_GROUP, group * MOBA_BLOCK, width), F32),
            pltpu.VMEM((2, HEADS_PER_GROUP, group * MOBA_BLOCK, width), qt.dtype),
        ],
        compiler_params=pltpu.CompilerParams(
            dimension_semantics=("arbitrary", "arbitrary", "arbitrary"), vmem_limit_bytes=VMEM_LIMIT),
        name="moba",
    )(qt, k, vt, kmean, onehot)


def _mix_kernel(cb_ref, cc_ref, cx_ref, cch_ref, cxh_ref, gc_ref, ga_ref, attn_ref, x_ref,
                convw_ref, wco_ref, wao_ref, wo_ref, g_ref, rw_ref, rb_ref, tri_ref,
                h_ref, hn_ref, route_ref, prob_ref, counts_ref, carry_ref, *, tm, tiles_per_seq):
    i = pl.program_id(0)
    act = wco_ref.dtype

    @pl.when(i == 0)
    def _():
        carry_ref[...] = jnp.zeros_like(carry_ref)

    u = cc_ref[...] * cx_ref[...]
    keep = (i % tiles_per_seq != 0).astype(F32)
    halo = cch_ref[...] * cxh_ref[...] * keep
    prev1, prev2 = halo[SUBLANES - 1:SUBLANES, :], halo[SUBLANES - 2:SUBLANES - 1, :]
    row = lax.broadcasted_iota(jnp.int32, (tm, 1), 0)
    u1 = jnp.where(row == 0, prev1, pltpu.roll(u, 1, axis=0))
    u2 = jnp.where(row == 0, prev2, jnp.where(row == 1, prev1, pltpu.roll(u, 2, axis=0)))
    cw = convw_ref[...]
    conv = cw[0:1, :] * u2 + cw[1:2, :] * u1 + cw[2:3, :] * u
    y_conv = _dot((cb_ref[...] * conv).astype(act), wco_ref[...])
    y_attn = _dot(attn_ref[...], wao_ref[...])
    merged = _sigmoid(gc_ref[...]) * y_conv + _sigmoid(ga_ref[...]) * y_attn
    h = x_ref[...] + _dot(merged.astype(act), wo_ref[...])
    h_ref[...] = h
    hn = (h * lax.rsqrt(jnp.mean(h * h, axis=-1, keepdims=True) + EPS)) * g_ref[...]
    hn_ref[...] = hn

    logits = _dot(hn.astype(act), rw_ref[...]) + rb_ref[...]
    lane = lax.broadcasted_iota(jnp.int32, (1, LANES), 1)
    work = logits
    picks, vals = [], []
    sel = jnp.zeros(logits.shape, jnp.bool_)
    for _ in range(TOP_K):
        top = jnp.max(work, axis=-1, keepdims=True)
        first = jnp.min(jnp.where(work == top, lane, LANES), axis=-1, keepdims=True)
        hit = lane == first
        sel = sel | hit
        work = jnp.where(hit, -jnp.inf, work)
        picks.append(first)
        vals.append(top)
    exps = [jnp.exp(v - vals[0]) for v in vals]
    denom = exps[0] + exps[1] + exps[2] + exps[3]

    sel_f = jnp.where(sel, 1.0, 0.0)
    before = _dot(tri_ref[...], sel_f.astype(act)) + carry_ref[0:1, :]
    carry_ref[...] = carry_ref[...] + jnp.sum(sel_f, axis=0, keepdims=True)
    counts_ref[...] = carry_ref[...]

    route = jnp.zeros(logits.shape, jnp.int32)
    prob = jnp.zeros(logits.shape, F32)
    for t in range(TOP_K):
        rank = jnp.sum(jnp.where(lane == picks[t], before, 0.0), axis=-1, keepdims=True)
        route = jnp.where(lane == t, picks[t], route)
        route = jnp.where(lane == TOP_K + t, rank.astype(jnp.int32), route)
        prob = jnp.where(lane == t, exps[t] / denom, prob)
    route_ref[...] = route
    prob_ref[...] = prob


def _mix(zc, gates, attn, x2, conv_w, w_conv_out, w_attn_out, w_out, g, router_w, router_b, tri,
         *, seq, tm):
    n = x2.shape[0]
    n_tiles = n // tm
    halo_blocks = tm // SUBLANES
    row_tile = lambda i: (i, 0)
    const = lambda i: (0, 0)
    col = lambda c: (lambda i: (i, c))
    halo = lambda c: (lambda i: (jnp.maximum(i * halo_blocks - 1, 0), c))
    full = lambda a: pl.BlockSpec(a.shape, const)
    out_shape = (
        jax.ShapeDtypeStruct((n, D_MODEL), F32),
        jax.ShapeDtypeStruct((n, D_MODEL), F32),
        jax.ShapeDtypeStruct((n, LANES), jnp.int32),
        jax.ShapeDtypeStruct((n, LANES), F32),
        jax.ShapeDtypeStruct((SUBLANES, LANES), F32),
    )
    return pl.pallas_call(
        functools.partial(_mix_kernel, tm=tm, tiles_per_seq=seq // tm),
        out_shape=out_shape,
        grid=(n_tiles,),
        in_specs=[
            pl.BlockSpec((tm, D_MODEL), col(0)), pl.BlockSpec((tm, D_MODEL), col(1)),
            pl.BlockSpec((tm, D_MODEL), col(2)),
            pl.BlockSpec((SUBLANES, D_MODEL), halo(1)), pl.BlockSpec((SUBLANES, D_MODEL), halo(2)),
            pl.BlockSpec((tm, D_MODEL), col(0)), pl.BlockSpec((tm, D_MODEL), col(1)),
            pl.BlockSpec((tm, D_MODEL), row_tile), pl.BlockSpec((tm, D_MODEL), row_tile),
            full(conv_w), full(w_conv_out), full(w_attn_out), full(w_out), full(g),
            full(router_w), full(router_b), full(tri),
        ],
        out_specs=(
            pl.BlockSpec((tm, D_MODEL), row_tile), pl.BlockSpec((tm, D_MODEL), row_tile),
            pl.BlockSpec((tm, LANES), row_tile), pl.BlockSpec((tm, LANES), row_tile),
            pl.BlockSpec((SUBLANES, LANES), const),
        ),
        scratch_shapes=[pltpu.VMEM((SUBLANES, LANES), F32)],
        compiler_params=pltpu.CompilerParams(
            dimension_semantics=("arbitrary",), vmem_limit_bytes=VMEM_LIMIT),
        name="mix",
    )(zc, zc, zc, zc, zc, gates, gates, attn, x2, conv_w, w_conv_out, w_attn_out, w_out, g,
      router_w, router_b, tri)


def _row_copy(src, src_row, dst, dst_row, sem):
    return pltpu.make_async_copy(src.at[pl.ds(src_row, 1), :], dst.at[pl.ds(dst_row, 1), :], sem)


def _start_rows(copy_of_row, n_rows):
    for r in range(n_rows):
        copy_of_row(r).start()


def _wait_rows(copy_of_row, n_rows):
    def wait(r, c):
        copy_of_row(r).wait()
        return c
    lax.fori_loop(0, n_rows, wait, 0, unroll=8)


def _dispatch_kernel(dst_ref, hn_ref, xs_hbm, stage, sem, *, per_step):
    i = pl.program_id(0)
    last = pl.num_programs(0) - 1
    slot = i % 2
    stage[slot] = hn_ref[...]
    _start_rows(lambda r: _row_copy(stage.at[slot], r // TOP_K, xs_hbm, dst_ref[0, 0, r], sem.at[slot]), per_step)

    @pl.when(i > 0)
    def _():
        _wait_rows(lambda r: _row_copy(stage.at[1 - slot], 0, xs_hbm, 0, sem.at[1 - slot]), per_step)

    @pl.when(i == last)
    def _():
        _wait_rows(lambda r: _row_copy(stage.at[slot], 0, xs_hbm, 0, sem.at[slot]), per_step)


def _dispatch(dst, hn, *, per_step):
    n_slots = dst.shape[0]
    steps = n_slots // per_step
    tokens = per_step // TOP_K
    last_block = hn.shape[0] // tokens - 1
    assert steps * per_step == n_slots and tokens * TOP_K == per_step
    return pl.pallas_call(
        functools.partial(_dispatch_kernel, per_step=per_step),
        out_shape=jax.ShapeDtypeStruct((n_slots, D_MODEL), hn.dtype),
        grid=(steps,),
        in_specs=[
            pl.BlockSpec((1, 1, per_step), lambda i: (i, 0, 0), memory_space=pltpu.SMEM),
            pl.BlockSpec((tokens, D_MODEL), lambda i: (jnp.minimum(i, last_block), 0)),
        ],
        out_specs=pl.BlockSpec(memory_space=pl.ANY),
        scratch_shapes=[pltpu.VMEM((2, tokens, D_MODEL), hn.dtype), pltpu.SemaphoreType.DMA((2,))],
        compiler_params=pltpu.CompilerParams(dimension_semantics=("arbitrary",)),
        name="dispatch",
    )(dst.reshape(steps, 1, per_step), hn)


def _experts_kernel(be_ref, nused_ref, xs_ref, wgu_ref, bgu_ref, wd_ref, bd_ref, ys_ref, wgu_act, wd_act):
    i = pl.program_id(0)
    act = wgu_act.dtype

    @pl.when((i == 0) | (be_ref[i] != be_ref[jnp.maximum(i - 1, 0)]))
    def _():
        wgu_act[...] = wgu_ref[0].astype(act)
        wd_act[...] = wd_ref[0].astype(act)

    @pl.when(i < nused_ref[0])
    def _():
        gu = _dot(xs_ref[...].astype(act), wgu_act[...]) + bgu_ref[0]
        gate = jnp.minimum(gu[:, :D_FF], SWIGLU_LIMIT)
        up = jnp.clip(gu[:, D_FF:], -SWIGLU_LIMIT, SWIGLU_LIMIT)
        a = (up + 1.0) * (gate * _sigmoid(SWIGLU_ALPHA * gate))
        ys_ref[...] = _dot(a.astype(act), wd_act[...]) + bd_ref[0]

    @pl.when(i >= nused_ref[0])
    def _():
        ys_ref[...] = jnp.zeros_like(ys_ref)


def _experts(block_expert, n_used, xs, w_gate_up, b_gate_up, w_down, b_down):
    n_blocks = block_expert.shape[0]
    row_blk = pl.BlockSpec((EXPERT_ROWS, D_MODEL), lambda i, be, nu: (i, 0))
    grid_spec = pltpu.PrefetchScalarGridSpec(
        num_scalar_prefetch=2,
        grid=(n_blocks,),
        in_specs=[
            row_blk,
            pl.BlockSpec((1, D_MODEL, 2 * D_FF), lambda i, be, nu: (be[i], 0, 0)),
            pl.BlockSpec((1, 1, 2 * D_FF), lambda i, be, nu: (be[i], 0, 0)),
            pl.BlockSpec((1, D_FF, D_MODEL), lambda i, be, nu: (be[i], 0, 0)),
            pl.BlockSpec((1, 1, D_MODEL), lambda i, be, nu: (be[i], 0, 0)),
        ],
        out_specs=row_blk,
        scratch_shapes=[
            pltpu.VMEM((D_MODEL, 2 * D_FF), _MXU_DTYPE),
            pltpu.VMEM((D_FF, D_MODEL), _MXU_DTYPE),
        ],
    )
    return pl.pallas_call(
        _experts_kernel,
        out_shape=jax.ShapeDtypeStruct((n_blocks * EXPERT_ROWS, D_MODEL), F32),
        grid_spec=grid_spec,
        compiler_params=pltpu.CompilerParams(
            dimension_semantics=("arbitrary",), vmem_limit_bytes=VMEM_LIMIT),
        name="experts",
    )(block_expert, n_used, xs, w_gate_up,
      b_gate_up.reshape(N_EXPERTS, 1, 2 * D_FF), w_down, b_down.reshape(N_EXPERTS, 1, D_MODEL))


def _combine_kernel(dest_ref, dest_next_ref, ys_hbm, h_ref, prob_ref, g_ref, o_ref, ybuf, sem, *, tm):
    i = pl.program_id(0)
    last = pl.num_programs(0) - 1
    slot = i % 2

    def start_gather(dests, s):
        for k in range(TOP_K):
            _start_rows(lambda r: _row_copy(ys_hbm, dests[0, 0, r * TOP_K + k], ybuf.at[s, k], r, sem.at[s]), tm)

    @pl.when(i == 0)
    def _():
        start_gather(dest_ref, 0)

    @pl.when(i < last)
    def _():
        start_gather(dest_next_ref, 1 - slot)

    for k in range(TOP_K):
        _wait_rows(lambda r: _row_copy(ys_hbm, 0, ybuf.at[slot, k], r, sem.at[slot]), tm)
    prob = prob_ref[...]
    h = h_ref[...]
    for k in range(TOP_K):
        h = h + prob[:, k:k + 1] * ybuf[slot, k]
    o_ref[...] = (h * lax.rsqrt(jnp.mean(h * h, axis=-1, keepdims=True) + EPS)) * g_ref[...]


def _combine(dest, ys, h, prob, g, *, tm):
    n = h.shape[0]
    n_tiles = n // tm
    dest3 = dest.reshape(n_tiles, 1, tm * TOP_K)
    smem_blk = lambda f: pl.BlockSpec((1, 1, tm * TOP_K), f, memory_space=pltpu.SMEM)
    row_tile = lambda i: (i, 0)
    return pl.pallas_call(
        functools.partial(_combine_kernel, tm=tm),
        out_shape=jax.ShapeDtypeStruct((n, D_MODEL), F32),
        grid=(n_tiles,),
        in_specs=[
            smem_blk(lambda i: (i, 0, 0)),
            smem_blk(lambda i: (jnp.minimum(i + 1, n_tiles - 1), 0, 0)),
            pl.BlockSpec(memory_space=pl.ANY),
            pl.BlockSpec((tm, D_MODEL), row_tile),
            pl.BlockSpec((tm, LANES), row_tile),
            pl.BlockSpec((1, D_MODEL), lambda i: (0, 0)),
        ],
        out_specs=pl.BlockSpec((tm, D_MODEL), row_tile),
        scratch_shapes=[
            pltpu.VMEM((2, TOP_K, tm, D_MODEL), F32),
            pltpu.SemaphoreType.DMA((2,)),
        ],
        compiler_params=pltpu.CompilerParams(
            dimension_semantics=("arbitrary",), vmem_limit_bytes=VMEM_LIMIT),
        name="combine",
    )(dest3, dest3, ys, h, prob, g)


def _rope_tables(seq):
    inv_freq = ROPE_THETA ** (-jnp.arange(0, HALF_DIM, dtype=F32) / HALF_DIM)
    ang = jnp.arange(seq, dtype=jnp.int32).astype(F32)[:, None] * inv_freq[None, :]
    cos, sin = jnp.cos(ang), jnp.sin(ang)
    reps = LANES // HEAD_DIM
    cos_t = jnp.tile(jnp.concatenate([cos, cos], axis=1), (1, reps))
    sin_t = jnp.tile(jnp.concatenate([-sin, sin], axis=1), (1, reps))
    return cos_t, sin_t


def _tile_rows(seq, want):
    tm = min(want, seq)
    assert seq % tm == 0 and tm % MOBA_BLOCK == 0, (seq, tm)
    return tm


def kernel(x, norm_mix_g, w_in, conv_w, w_conv_out, w_attn_out, w_out, norm_ffn_g, router_w, router_b,
           w_gate_up, b_gate_up, w_down, b_down, norm_final_g):
    batch, seq, d = x.shape
    assert d == D_MODEL and seq % MOBA_BLOCK == 0
    n = batch * seq
    n_kb = seq // MOBA_BLOCK
    act = _MXU_DTYPE
    x2 = x.reshape(n, d)
    tm_proj = _tile_rows(seq, 512)
    tm_mix = _tile_rows(seq, 256)
    tm_out = _tile_rows(seq, 256)

    cos_t, sin_t = _rope_tables(seq)
    zc, qt, k, vt, gates, kmean = _inproj(
        x2, norm_mix_g.reshape(1, d), w_in.astype(act), cos_t, sin_t, seq=seq, tm=tm_proj)
    attn = _moba(qt, k, vt, kmean.reshape(batch, n_kb, d), batch=batch, n_kb=n_kb)

    router_w_pad = jnp.zeros((d, LANES), F32).at[:, :N_EXPERTS].set(router_w).astype(act)
    router_b_pad = jnp.full((1, LANES), MASKED, F32).at[0, :N_EXPERTS].set(router_b)
    tri = (jnp.arange(tm_mix)[:, None] > jnp.arange(tm_mix)[None, :]).astype(act)
    h, hn, route, prob, counts = _mix(
        zc, gates, attn, x2, conv_w, w_conv_out.astype(act), w_attn_out.astype(act), w_out.astype(act),
        norm_ffn_g.reshape(1, d), router_w_pad, router_b_pad, tri, seq=seq, tm=tm_mix)

    nk = n * TOP_K
    n_slots = ((nk + N_EXPERTS * (EXPERT_ROWS - 1) + EXPERT_ROWS - 1) // EXPERT_ROWS) * EXPERT_ROWS
    n_blocks = n_slots // EXPERT_ROWS
    cnt = counts[0, :N_EXPERTS].astype(jnp.int32)
    padded = ((cnt + EXPERT_ROWS - 1) // EXPERT_ROWS) * EXPERT_ROWS
    padded_end = jnp.cumsum(padded)
    start_pad = padded_end - padded
    expert = route[:, :TOP_K]
    rank = route[:, TOP_K:2 * TOP_K]
    dest = (start_pad[expert] + rank).reshape(nk)
    n_pad = n_slots - nk
    pad_cnt = jnp.concatenate([padded - cnt, n_slots - padded_end[-1:]])
    pad_end = jnp.cumsum(pad_cnt)
    pad_first_slot = jnp.concatenate([start_pad + cnt, padded_end[-1:]])
    pad_idx = jnp.arange(n_pad, dtype=jnp.int32)
    pad_owner = jnp.sum((pad_end[None, :] <= pad_idx[:, None]).astype(jnp.int32), axis=1)
    pad_slot = pad_first_slot[pad_owner] + pad_idx - (pad_end - pad_cnt)[pad_owner]
    dst_slot = jnp.concatenate([dest, pad_slot])
    block_start = jnp.arange(n_blocks, dtype=jnp.int32) * EXPERT_ROWS
    block_expert = jnp.minimum(
        jnp.sum((padded_end[None, :] <= block_start[:, None]).astype(jnp.int32), axis=1), N_EXPERTS - 1)
    n_used = (padded_end[-1:] // EXPERT_ROWS).astype(jnp.int32)

    xs = _dispatch(dst_slot, hn, per_step=EXPERT_ROWS)
    ys = _experts(block_expert, n_used, xs, w_gate_up, b_gate_up, w_down, b_down)
    out = _combine(dest, ys, h, prob, norm_final_g.reshape(1, d), tm=tm_out)
    return out.reshape(batch, seq, d)
```

```python
import functools

import jax
import jax.numpy as jnp
from jax import lax
from jax.experimental import pallas as pl
from jax.experimental.pallas import tpu as pltpu

D_MODEL = 1024
N_HEADS = 16
HEAD_DIM = 64
HALF_DIM = HEAD_DIM // 2
CONV_WIDTH = 3
MOBA_BLOCK = 256
TOP_BLOCKS = 3
ROPE_THETA = 10000.0
N_EXPERTS = 32
TOP_K = 4
D_FF = D_MODEL
SWIGLU_LIMIT = 7.0
SWIGLU_ALPHA = 1.702
EXPERT_ROWS = 512
EPS = 1e-5
IN_PROJ_CHUNKS = 8

LANES = 128
SUBLANES = 8
HEADS_PER_GROUP = LANES // HEAD_DIM
N_HEAD_GROUPS = N_HEADS // HEADS_PER_GROUP
MOBA_GROUP = 4
MOBA_QBLOCKS = 4
LOG2_E = 1.4426950408889634
MAX_LAGGED_RISE = 60.0
MASKED = -1e30
VMEM_LIMIT = 56 * 1024 * 1024

_MXU_DTYPE = jnp.bfloat16
F32 = jnp.float32


def _dot(a, b):
    return jnp.dot(a, b, preferred_element_type=F32)


def _sigmoid(t):
    return 1.0 / (1.0 + jnp.exp(-t))


def _rope_cols(z, cos, sin_signed):
    lane = lax.broadcasted_iota(jnp.int32, (1, LANES), 1)
    first_half = (lane % HEAD_DIM) < HALF_DIM
    cols = []
    for g in range(D_MODEL // LANES):
        t = z[:, g * LANES:(g + 1) * LANES]
        from_above = pltpu.roll(t, LANES - HALF_DIM, axis=1)
        from_below = pltpu.roll(t, HALF_DIM, axis=1)
        cols.append(t * cos + jnp.where(first_half, from_above, from_below) * sin_signed)
    return jnp.concatenate(cols, axis=1)


def _inproj_kernel(x_ref, g_ref, w_ref, cos_ref, sin_ref,
                   zc_ref, qt_ref, k_ref, vt_ref, gates_ref, kmean_ref, xn_ref, *, tm):
    j = pl.program_id(1)
    kb_per_tile = tm // MOBA_BLOCK

    @pl.when(j == 0)
    def _():
        x = x_ref[...]
        ms = jnp.mean(x * x, axis=-1, keepdims=True)
        xn_ref[...] = ((x * lax.rsqrt(ms + EPS)) * g_ref[...]).astype(xn_ref.dtype)

    z = _dot(xn_ref[...], w_ref[...])

    @pl.when(j < 3)
    def _():
        zc_ref[...] = z

    @pl.when(j == 3)
    def _():
        qt = (_rope_cols(z, cos_ref[...], sin_ref[...]) * (LOG2_E * HEAD_DIM ** -0.5)).T
        for c in range(kb_per_tile):
            qt_ref[c] = qt[:, c * MOBA_BLOCK:(c + 1) * MOBA_BLOCK].astype(qt_ref.dtype)

    @pl.when(j == 4)
    def _():
        kr = _rope_cols(z, cos_ref[...], sin_ref[...])
        for c in range(kb_per_tile):
            blk = kr[c * MOBA_BLOCK:(c + 1) * MOBA_BLOCK, :]
            k_ref[c] = blk.astype(k_ref.dtype)
            kmean_ref[0, c:c + 1, :] = jnp.mean(blk, axis=0, keepdims=True)

    @pl.when(j == 5)
    def _():
        vt = z.T
        for c in range(kb_per_tile):
            vt_ref[c] = vt[:, c * MOBA_BLOCK:(c + 1) * MOBA_BLOCK].astype(vt_ref.dtype)

    @pl.when(j >= 6)
    def _():
        gates_ref[...] = z


def _inproj(x2, g, w_in, cos, sin_signed, *, seq, tm):
    n = x2.shape[0]
    n_tiles = n // tm
    tiles_per_seq = seq // tm
    kb_per_tile = tm // MOBA_BLOCK
    n_kb = n // MOBA_BLOCK
    act = _MXU_DTYPE
    out_shape = (
        jax.ShapeDtypeStruct((n, 3 * D_MODEL), F32),
        jax.ShapeDtypeStruct((n_kb, D_MODEL, MOBA_BLOCK), act),
        jax.ShapeDtypeStruct((n_kb, MOBA_BLOCK, D_MODEL), act),
        jax.ShapeDtypeStruct((n_kb, D_MODEL, MOBA_BLOCK), act),
        jax.ShapeDtypeStruct((n, 2 * D_MODEL), F32),
        jax.ShapeDtypeStruct((n_tiles, kb_per_tile, D_MODEL), F32),
    )
    row_tile = lambda i, j: (i, 0)
    blk3 = lambda i, j: (i, 0, 0)
    return pl.pallas_call(
        functools.partial(_inproj_kernel, tm=tm),
        out_shape=out_shape,
        grid=(n_tiles, IN_PROJ_CHUNKS),
        in_specs=[
            pl.BlockSpec((tm, D_MODEL), row_tile),
            pl.BlockSpec((1, D_MODEL), lambda i, j: (0, 0)),
            pl.BlockSpec((D_MODEL, D_MODEL), lambda i, j: (0, j)),
            pl.BlockSpec((tm, LANES), lambda i, j: (i % tiles_per_seq, 0)),
            pl.BlockSpec((tm, LANES), lambda i, j: (i % tiles_per_seq, 0)),
        ],
        out_specs=(
            pl.BlockSpec((tm, D_MODEL), lambda i, j: (i, jnp.minimum(j, 2))),
            pl.BlockSpec((kb_per_tile, D_MODEL, MOBA_BLOCK), blk3),
            pl.BlockSpec((kb_per_tile, MOBA_BLOCK, D_MODEL), blk3),
            pl.BlockSpec((kb_per_tile, D_MODEL, MOBA_BLOCK), blk3),
            pl.BlockSpec((tm, D_MODEL), lambda i, j: (i, jnp.clip(j - 6, 0, 1))),
            pl.BlockSpec((1, kb_per_tile, D_MODEL), blk3),
        ),
        scratch_shapes=[pltpu.VMEM((tm, D_MODEL), act)],
        compiler_params=pltpu.CompilerParams(
            dimension_semantics=("arbitrary", "arbitrary"), vmem_limit_bytes=VMEM_LIMIT),
        name="inproj",
    )(x2, g, w_in, cos, sin_signed)


def _moba_kernel(qt_ref, k_ref, vt_ref, kmean_ref, onehot_ref, o_ref, qaug_ref, s_ref, p_ref,
                 *, n_kb, group, qpb):
    qb0 = pl.program_id(2) * qpb
    act = qt_ref.dtype
    blk = MOBA_BLOCK
    width = qpb * blk
    heads = range(HEADS_PER_GROUP)
    lanes_of = lambda c: slice(c * blk, (c + 1) * blk)
    qt = jnp.concatenate([qt_ref[c] for c in range(qpb)], axis=1)
    km = kmean_ref[0].astype(act)
    dim_row = lax.broadcasted_iota(jnp.int32, (LANES, 1), 0)
    kb_row = lax.broadcasted_iota(jnp.int32, (n_kb, 1), 0)
    lane_qb = qb0 + lax.broadcasted_iota(jnp.int32, (1, width), 1) // blk
    n_bias_rows = LANES

    for h in heads:
        in_head = (dim_row >= h * HEAD_DIM) & (dim_row < (h + 1) * HEAD_DIM)
        qh = jnp.where(in_head, qt, jnp.zeros_like(qt))
        gate = _dot(km, qh)
        work = jnp.where(kb_row < lane_qb, gate, -jnp.inf)
        sel = jnp.zeros(gate.shape, jnp.bool_)
        for _ in range(TOP_BLOCKS):
            top = jnp.max(work, axis=0, keepdims=True)
            first = jnp.min(jnp.where(work == top, kb_row, n_kb), axis=0, keepdims=True)
            hit = kb_row == first
            sel = sel | (hit & (top > -jnp.inf))
            work = jnp.where(hit, -jnp.inf, work)
        bias = jnp.where(sel, 0.0, MASKED).astype(act)
        pad = jnp.zeros((n_bias_rows - n_kb, width), act)
        qaug_ref[h] = jnp.concatenate([qh, bias, pad], axis=0)

    def v_rows(kb, h):
        return vt_ref[kb, h * HEAD_DIM:(h + 1) * HEAD_DIM, :]

    key_idx = lax.broadcasted_iota(jnp.int32, (blk, blk), 0)
    qry_idx = lax.broadcasted_iota(jnp.int32, (blk, blk), 1)
    causal = key_idx <= qry_idx
    ones_own = jnp.ones((2 * SUBLANES, blk), act)

    def own_scores(h, c):
        k_own = jnp.concatenate([k_ref[qb0 + c], jnp.zeros((blk, LANES), act)], axis=1)
        return jnp.where(causal, _dot(k_own, qaug_ref[h, :, lanes_of(c)]), MASKED)

    def own_pv(h, c, p):
        return _dot(jnp.concatenate([v_rows(qb0 + c, h), ones_own], axis=0), p.astype(act))

    rows = group * blk
    n_groups = jnp.maximum((qb0 + qpb - 1 + group - 1) // group, 1)
    ones_rows = jnp.ones((2 * SUBLANES, rows), act)

    def k_operand(j):
        kb0 = pl.multiple_of(j * group, group)
        k_big = k_ref[pl.ds(kb0, group)].reshape(rows, LANES)
        pick = onehot_ref[pl.ds(kb0, group)].reshape(rows, LANES)
        return jnp.concatenate([k_big, pick], axis=1)

    def v_operand(j, h):
        kb0 = pl.multiple_of(j * group, group)
        v_big = jnp.concatenate([v_rows(kb0 + u, h) for u in range(group)], axis=1)
        return jnp.concatenate([v_big, ones_rows], axis=0)

    def write_out(accs):
        outs = [acc[:HEAD_DIM] / acc[HEAD_DIM:HEAD_DIM + 1] for acc in accs]
        o_ref[...] = jnp.concatenate(outs, axis=0).T.astype(o_ref.dtype)

    def score_stage(j, refs, slot):
        s = [_dot(k_operand(j), qaug_ref[h]) for h in heads]
        tops = []
        for h in heads:
            p_ref[slot, h] = jnp.exp2(s[h] - refs[h]).astype(act)
            tops.append(jnp.max(s[h], axis=0, keepdims=True))
        return tops

    def pv_stage(j, slot):
        return [_dot(v_operand(j, h), p_ref[slot, h]) for h in heads]

    def pipelined_body(j, carry):
        pv = pv_stage(j - 1, (j - 1) % 2)
        tops = score_stage(j, [carry[5 * h + 1] for h in heads], j % 2)
        new = []
        for h in heads:
            ref_prev, ref, acc, rise, peak = carry[5 * h:5 * h + 5]
            new += [ref, jnp.maximum(ref, tops[h]), (acc + pv[h]) * jnp.exp2(ref_prev - ref),
                    jnp.maximum(rise, tops[h] - ref), jnp.maximum(peak, tops[h])]
        return tuple(new)

    zero = jnp.zeros((1, width), F32)
    tops = score_stage(0, [zero for _ in heads], 0)
    init = []
    for h in heads:
        init += [zero, jnp.maximum(zero, tops[h]), jnp.zeros((HEAD_DIM + 2 * SUBLANES, width), F32),
                 tops[h], tops[h]]
    fast = lax.fori_loop(1, n_groups, pipelined_body, tuple(init))
    pv = pv_stage(n_groups - 1, (n_groups - 1) % 2)
    accs, unsafe = [], []
    for h in heads:
        ref_prev, ref, acc, rise, peak = fast[5 * h:5 * h + 5]
        past = (acc + pv[h]) * jnp.exp2(ref_prev - ref)
        pieces = []
        for c in range(qpb):
            s = own_scores(h, c)
            top = jnp.max(s, axis=0, keepdims=True)
            ref_c = ref[:, lanes_of(c)]
            pieces.append(past[:, lanes_of(c)] + own_pv(h, c, jnp.exp2(s - ref_c)))
            unsafe.append(jnp.maximum(jnp.maximum(rise[:, lanes_of(c)], top - ref_c),
                                      -jnp.maximum(peak[:, lanes_of(c)], top)))
        accs.append(jnp.concatenate(pieces, axis=1))
    write_out(accs)
    worst = jnp.max(functools.reduce(jnp.maximum, unsafe))

    @pl.when(worst > MAX_LAGGED_RISE)
    def _():
        own = []
        for h in heads:
            ms, pvs = [], []
            for c in range(qpb):
                s = own_scores(h, c)
                m = jnp.max(s, axis=0, keepdims=True)
                ms.append(m)
                pvs.append(own_pv(h, c, jnp.exp2(s - m)))
            own += [jnp.concatenate(ms, axis=1), jnp.concatenate(pvs, axis=1)]

        def exact_body(j, carry):
            k_aug = k_operand(j)
            new = []
            for h in heads:
                m, acc = carry[2 * h], carry[2 * h + 1]
                s_ref[h] = _dot(k_aug, qaug_ref[h])
                m_new = jnp.maximum(m, jnp.max(s_ref[h], axis=0, keepdims=True))
                p_ref[0, h] = jnp.exp2(s_ref[h] - m_new).astype(act)
                new += [m_new, jnp.exp2(m - m_new) * acc + _dot(v_operand(j, h), p_ref[0, h])]
            return tuple(new)

        safe = lax.fori_loop(0, n_groups, exact_body, tuple(own))
        write_out([safe[2 * h + 1] for h in heads])


def _moba(qt, k, vt, kmean, *, batch, n_kb):
    n = batch * n_kb * MOBA_BLOCK
    group = MOBA_GROUP if n_kb % MOBA_GROUP == 0 else 1
    qpb = MOBA_QBLOCKS if n_kb % MOBA_QBLOCKS == 0 else 1
    width = qpb * MOBA_BLOCK
    steps = n_kb // qpb
    onehot = jnp.arange(LANES)[None, None, :] == jnp.arange(n_kb)[:, None, None]
    onehot = jnp.broadcast_to(onehot, (n_kb, MOBA_BLOCK, LANES)).astype(qt.dtype)
    return pl.pallas_call(
        functools.partial(_moba_kernel, n_kb=n_kb, group=group, qpb=qpb),
        out_shape=jax.ShapeDtypeStruct((n, D_MODEL), qt.dtype),
        grid=(batch, N_HEAD_GROUPS, steps),
        in_specs=[
            pl.BlockSpec((qpb, LANES, MOBA_BLOCK), lambda b, g, q: (b * steps + q, g, 0)),
            pl.BlockSpec((n_kb, MOBA_BLOCK, LANES), lambda b, g, q: (b, 0, g)),
            pl.BlockSpec((n_kb, LANES, MOBA_BLOCK), lambda b, g, q: (b, g, 0)),
            pl.BlockSpec((1, n_kb, LANES), lambda b, g, q: (b, 0, g)),
            pl.BlockSpec((n_kb, MOBA_BLOCK, LANES), lambda b, g, q: (0, 0, 0)),
        ],
        out_specs=pl.BlockSpec((width, LANES), lambda b, g, q: (b * steps + q, g)),
        scratch_shapes=[
            pltpu.VMEM((HEADS_PER_GROUP, 2 * LANES, width), qt.dtype),
            pltpu.VMEM((HEADS_PER_GROUP, group * MOBA_BLOCK, width), F32),
            pltpu.VMEM((2, HEADS_PER_GROUP, group * MOBA_BLOCK, width), qt.dtype),
        ],
        compiler_params=pltpu.CompilerParams(
            dimension_semantics=("arbitrary", "arbitrary", "arbitrary"), vmem_limit_bytes=VMEM_LIMIT),
        name="moba",
    )(qt, k, vt, kmean, onehot)


def _mix_kernel(cb_ref, cc_ref, cx_ref, cch_ref, cxh_ref, gc_ref, ga_ref, attn_ref, x_ref,
                convw_ref, wco_ref, wao_ref, wo_ref, g_ref, rw_ref, rb_ref, tri_ref,
                h_ref, hn_ref, route_ref, prob_ref, counts_ref, carry_ref, *, tm, tiles_per_seq):
    i = pl.program_id(0)
    act = wco_ref.dtype

    @pl.when(i == 0)
    def _():
        carry_ref[...] = jnp.zeros_like(carry_ref)

    u = cc_ref[...] * cx_ref[...]
    keep = (i % tiles_per_seq != 0).astype(F32)
    halo = cch_ref[...] * cxh_ref[...] * keep
    prev1, prev2 = halo[SUBLANES - 1:SUBLANES, :], halo[SUBLANES - 2:SUBLANES - 1, :]
    row = lax.broadcasted_iota(jnp.int32, (tm, 1), 0)
    u1 = jnp.where(row == 0, prev1, pltpu.roll(u, 1, axis=0))
    u2 = jnp.where(row == 0, prev2, jnp.where(row == 1, prev1, pltpu.roll(u, 2, axis=0)))
    cw = convw_ref[...]
    conv = cw[0:1, :] * u2 + cw[1:2, :] * u1 + cw[2:3, :] * u
    y_conv = _dot((cb_ref[...] * conv).astype(act), wco_ref[...])
    y_attn = _dot(attn_ref[...], wao_ref[...])
    merged = _sigmoid(gc_ref[...]) * y_conv + _sigmoid(ga_ref[...]) * y_attn
    h = x_ref[...] + _dot(merged.astype(act), wo_ref[...])
    h_ref[...] = h
    hn = (h * lax.rsqrt(jnp.mean(h * h, axis=-1, keepdims=True) + EPS)) * g_ref[...]
    hn_ref[...] = hn

    logits = _dot(hn.astype(act), rw_ref[...]) + rb_ref[...]
    lane = lax.broadcasted_iota(jnp.int32, (1, LANES), 1)
    work = logits
    picks, vals = [], []
    sel = jnp.zeros(logits.shape, jnp.bool_)
    for _ in range(TOP_K):
        top = jnp.max(work, axis=-1, keepdims=True)
        first = jnp.min(jnp.where(work == top, lane, LANES), axis=-1, keepdims=True)
        hit = lane == first
        sel = sel | hit
        work = jnp.where(hit, -jnp.inf, work)
        picks.append(first)
        vals.append(top)
    exps = [jnp.exp(v - vals[0]) for v in vals]
    denom = exps[0] + exps[1] + exps[2] + exps[3]

    sel_f = jnp.where(sel, 1.0, 0.0)
    before = _dot(tri_ref[...], sel_f.astype(act)) + carry_ref[0:1, :]
    carry_ref[...] = carry_ref[...] + jnp.sum(sel_f, axis=0, keepdims=True)
    counts_ref[...] = carry_ref[...]

    route = jnp.zeros(logits.shape, jnp.int32)
    prob = jnp.zeros(logits.shape, F32)
    for t in range(TOP_K):
        rank = jnp.sum(jnp.where(lane == picks[t], before, 0.0), axis=-1, keepdims=True)
        route = jnp.where(lane == t, picks[t], route)
        route = jnp.where(lane == TOP_K + t, rank.astype(jnp.int32), route)
        prob = jnp.where(lane == t, exps[t] / denom, prob)
    route_ref[...] = route
    prob_ref[...] = prob


def _mix(zc, gates, attn, x2, conv_w, w_conv_out, w_attn_out, w_out, g, router_w, router_b, tri,
         *, seq, tm):
    n = x2.shape[0]
    n_tiles = n // tm
    halo_blocks = tm // SUBLANES
    row_tile = lambda i: (i, 0)
    const = lambda i: (0, 0)
    col = lambda c: (lambda i: (i, c))
    halo = lambda c: (lambda i: (jnp.maximum(i * halo_blocks - 1, 0), c))
    full = lambda a: pl.BlockSpec(a.shape, const)
    out_shape = (
        jax.ShapeDtypeStruct((n, D_MODEL), F32),
        jax.ShapeDtypeStruct((n, D_MODEL), F32),
        jax.ShapeDtypeStruct((n, LANES), jnp.int32),
        jax.ShapeDtypeStruct((n, LANES), F32),
        jax.ShapeDtypeStruct((SUBLANES, LANES), F32),
    )
    return pl.pallas_call(
        functools.partial(_mix_kernel, tm=tm, tiles_per_seq=seq // tm),
        out_shape=out_shape,
        grid=(n_tiles,),
        in_specs=[
            pl.BlockSpec((tm, D_MODEL), col(0)), pl.BlockSpec((tm, D_MODEL), col(1)),
            pl.BlockSpec((tm, D_MODEL), col(2)),
            pl.BlockSpec((SUBLANES, D_MODEL), halo(1)), pl.BlockSpec((SUBLANES, D_MODEL), halo(2)),
            pl.BlockSpec((tm, D_MODEL), col(0)), pl.BlockSpec((tm, D_MODEL), col(1)),
            pl.BlockSpec((tm, D_MODEL), row_tile), pl.BlockSpec((tm, D_MODEL), row_tile),
            full(conv_w), full(w_conv_out), full(w_attn_out), full(w_out), full(g),
            full(router_w), full(router_b), full(tri),
        ],
        out_specs=(
            pl.BlockSpec((tm, D_MODEL), row_tile), pl.BlockSpec((tm, D_MODEL), row_tile),
            pl.BlockSpec((tm, LANES), row_tile), pl.BlockSpec((tm, LANES), row_tile),
            pl.BlockSpec((SUBLANES, LANES), const),
        ),
        scratch_shapes=[pltpu.VMEM((SUBLANES, LANES), F32)],
        compiler_params=pltpu.CompilerParams(
            dimension_semantics=("arbitrary",), vmem_limit_bytes=VMEM_LIMIT),
        name="mix",
    )(zc, zc, zc, zc, zc, gates, gates, attn, x2, conv_w, w_conv_out, w_attn_out, w_out, g,
      router_w, router_b, tri)


def _row_copy(src, src_row, dst, dst_row, sem):
    return pltpu.make_async_copy(src.at[pl.ds(src_row, 1), :], dst.at[pl.ds(dst_row, 1), :], sem)


def _start_rows(copy_of_row, n_rows):
    for r in range(n_rows):
        copy_of_row(r).start()


def _wait_rows(copy_of_row, n_rows):
    def wait(r, c):
        copy_of_row(r).wait()
        return c
    lax.fori_loop(0, n_rows, wait, 0, unroll=8)


def _dispatch_kernel(dst_ref, hn_ref, xs_hbm, stage, sem, *, per_step):
    i = pl.program_id(0)
    last = pl.num_programs(0) - 1
    slot = i % 2
    stage[slot] = hn_ref[...]
    _start_rows(lambda r: _row_copy(stage.at[slot], r // TOP_K, xs_hbm, dst_ref[0, 0, r], sem.at[slot]), per_step)

    @pl.when(i > 0)
    def _():
        _wait_rows(lambda r: _row_copy(stage.at[1 - slot], 0, xs_hbm, 0, sem.at[1 - slot]), per_step)

    @pl.when(i == last)
    def _():
        _wait_rows(lambda r: _row_copy(stage.at[slot], 0, xs_hbm, 0, sem.at[slot]), per_step)


def _dispatch(dst, hn, *, per_step):
    n_slots = dst.shape[0]
    steps = n_slots // per_step
    tokens = per_step // TOP_K
    last_block = hn.shape[0] // tokens - 1
    assert steps * per_step == n_slots and tokens * TOP_K == per_step
    return pl.pallas_call(
        functools.partial(_dispatch_kernel, per_step=per_step),
        out_shape=jax.ShapeDtypeStruct((n_slots, D_MODEL), hn.dtype),
        grid=(steps,),
        in_specs=[
            pl.BlockSpec((1, 1, per_step), lambda i: (i, 0, 0), memory_space=pltpu.SMEM),
            pl.BlockSpec((tokens, D_MODEL), lambda i: (jnp.minimum(i, last_block), 0)),
        ],
        out_specs=pl.BlockSpec(memory_space=pl.ANY),
        scratch_shapes=[pltpu.VMEM((2, tokens, D_MODEL), hn.dtype), pltpu.SemaphoreType.DMA((2,))],
        compiler_params=pltpu.CompilerParams(dimension_semantics=("arbitrary",)),
        name="dispatch",
    )(dst.reshape(steps, 1, per_step), hn)


def _experts_kernel(be_ref, nused_ref, xs_ref, wgu_ref, bgu_ref, wd_ref, bd_ref, ys_ref, wgu_act, wd_act):
    i = pl.program_id(0)
    act = wgu_act.dtype

    @pl.when((i == 0) | (be_ref[i] != be_ref[jnp.maximum(i - 1, 0)]))
    def _():
        wgu_act[...] = wgu_ref[0].astype(act)
        wd_act[...] = wd_ref[0].astype(act)

    @pl.when(i < nused_ref[0])
    def _():
        gu = _dot(xs_ref[...].astype(act), wgu_act[...]) + bgu_ref[0]
        gate = jnp.minimum(gu[:, :D_FF], SWIGLU_LIMIT)
        up = jnp.clip(gu[:, D_FF:], -SWIGLU_LIMIT, SWIGLU_LIMIT)
        a = (up + 1.0) * (gate * _sigmoid(SWIGLU_ALPHA * gate))
        ys_ref[...] = _dot(a.astype(act), wd_act[...]) + bd_ref[0]

    @pl.when(i >= nused_ref[0])
    def _():
        ys_ref[...] = jnp.zeros_like(ys_ref)


def _experts(block_expert, n_used, xs, w_gate_up, b_gate_up, w_down, b_down):
    n_blocks = block_expert.shape[0]
    row_blk = pl.BlockSpec((EXPERT_ROWS, D_MODEL), lambda i, be, nu: (i, 0))
    grid_spec = pltpu.PrefetchScalarGridSpec(
        num_scalar_prefetch=2,
        grid=(n_blocks,),
        in_specs=[
            row_blk,
            pl.BlockSpec((1, D_MODEL, 2 * D_FF), lambda i, be, nu: (be[i], 0, 0)),
            pl.BlockSpec((1, 1, 2 * D_FF), lambda i, be, nu: (be[i], 0, 0)),
            pl.BlockSpec((1, D_FF, D_MODEL), lambda i, be, nu: (be[i], 0, 0)),
            pl.BlockSpec((1, 1, D_MODEL), lambda i, be, nu: (be[i], 0, 0)),
        ],
        out_specs=row_blk,
        scratch_shapes=[
            pltpu.VMEM((D_MODEL, 2 * D_FF), _MXU_DTYPE),
            pltpu.VMEM((D_FF, D_MODEL), _MXU_DTYPE),
        ],
    )
    return pl.pallas_call(
        _experts_kernel,
        out_shape=jax.ShapeDtypeStruct((n_blocks * EXPERT_ROWS, D_MODEL), F32),
        grid_spec=grid_spec,
        compiler_params=pltpu.CompilerParams(
            dimension_semantics=("arbitrary",), vmem_limit_bytes=VMEM_LIMIT),
        name="experts",
    )(block_expert, n_used, xs, w_gate_up,
      b_gate_up.reshape(N_EXPERTS, 1, 2 * D_FF), w_down, b_down.reshape(N_EXPERTS, 1, D_MODEL))


def _combine_kernel(dest_ref, dest_next_ref, ys_hbm, h_ref, prob_ref, g_ref, o_ref, ybuf, sem, *, tm):
    i = pl.program_id(0)
    last = pl.num_programs(0) - 1
    slot = i % 2

    def start_gather(dests, s):
        for k in range(TOP_K):
            _start_rows(lambda r: _row_copy(ys_hbm, dests[0, 0, r * TOP_K + k], ybuf.at[s, k], r, sem.at[s]), tm)

    @pl.when(i == 0)
    def _():
        start_gather(dest_ref, 0)

    @pl.when(i < last)
    def _():
        start_gather(dest_next_ref, 1 - slot)

    for k in range(TOP_K):
        _wait_rows(lambda r: _row_copy(ys_hbm, 0, ybuf.at[slot, k], r, sem.at[slot]), tm)
    prob = prob_ref[...]
    h = h_ref[...]
    for k in range(TOP_K):
        h = h + prob[:, k:k + 1] * ybuf[slot, k]
    o_ref[...] = (h * lax.rsqrt(jnp.mean(h * h, axis=-1, keepdims=True) + EPS)) * g_ref[...]


def _combine(dest, ys, h, prob, g, *, tm):
    n = h.shape[0]
    n_tiles = n // tm
    dest3 = dest.reshape(n_tiles, 1, tm * TOP_K)
    smem_blk = lambda f: pl.BlockSpec((1, 1, tm * TOP_K), f, memory_space=pltpu.SMEM)
    row_tile = lambda i: (i, 0)
    return pl.pallas_call(
        functools.partial(_combine_kernel, tm=tm),
        out_shape=jax.ShapeDtypeStruct((n, D_MODEL), F32),
        grid=(n_tiles,),
        in_specs=[
            smem_blk(lambda i: (i, 0, 0)),
            smem_blk(lambda i: (jnp.minimum(i + 1, n_tiles - 1), 0, 0)),
            pl.BlockSpec(memory_space=pl.ANY),
            pl.BlockSpec((tm, D_MODEL), row_tile),
            pl.BlockSpec((tm, LANES), row_tile),
            pl.BlockSpec((1, D_MODEL), lambda i: (0, 0)),
        ],
        out_specs=pl.BlockSpec((tm, D_MODEL), row_tile),
        scratch_shapes=[
            pltpu.VMEM((2, TOP_K, tm, D_MODEL), F32),
            pltpu.SemaphoreType.DMA((2,)),
        ],
        compiler_params=pltpu.CompilerParams(
            dimension_semantics=("arbitrary",), vmem_limit_bytes=VMEM_LIMIT),
        name="combine",
    )(dest3, dest3, ys, h, prob, g)


def _rope_tables(seq):
    inv_freq = ROPE_THETA ** (-jnp.arange(0, HALF_DIM, dtype=F32) / HALF_DIM)
    ang = jnp.arange(seq, dtype=jnp.int32).astype(F32)[:, None] * inv_freq[None, :]
    cos, sin = jnp.cos(ang), jnp.sin(ang)
    reps = LANES // HEAD_DIM
    cos_t = jnp.tile(jnp.concatenate([cos, cos], axis=1), (1, reps))
    sin_t = jnp.tile(jnp.concatenate([-sin, sin], axis=1), (1, reps))
    return cos_t, sin_t


def _tile_rows(seq, want):
    tm = min(want, seq)
    assert seq % tm == 0 and tm % MOBA_BLOCK == 0, (seq, tm)
    return tm


def kernel(x, norm_mix_g, w_in, conv_w, w_conv_out, w_attn_out, w_out, norm_ffn_g, router_w, router_b,
           w_gate_up, b_gate_up, w_down, b_down, norm_final_g):
    batch, seq, d = x.shape
    assert d == D_MODEL and seq % MOBA_BLOCK == 0
    n = batch * seq
    n_kb = seq // MOBA_BLOCK
    act = _MXU_DTYPE
    x2 = x.reshape(n, d)
    tm_proj = _tile_rows(seq, 512)
    tm_mix = _tile_rows(seq, 256)
    tm_out = _tile_rows(seq, 256)

    cos_t, sin_t = _rope_tables(seq)
    zc, qt, k, vt, gates, kmean = _inproj(
        x2, norm_mix_g.reshape(1, d), w_in.astype(act), cos_t, sin_t, seq=seq, tm=tm_proj)
    attn = _moba(qt, k, vt, kmean.reshape(batch, n_kb, d), batch=batch, n_kb=n_kb)

    router_w_pad = jnp.zeros((d, LANES), F32).at[:, :N_EXPERTS].set(router_w).astype(act)
    router_b_pad = jnp.full((1, LANES), MASKED, F32).at[0, :N_EXPERTS].set(router_b)
    tri = (jnp.arange(tm_mix)[:, None] > jnp.arange(tm_mix)[None, :]).astype(act)
    h, hn, route, prob, counts = _mix(
        zc, gates, attn, x2, conv_w, w_conv_out.astype(act), w_attn_out.astype(act), w_out.astype(act),
        norm_ffn_g.reshape(1, d), router_w_pad, router_b_pad, tri, seq=seq, tm=tm_mix)

    nk = n * TOP_K
    n_slots = ((nk + N_EXPERTS * (EXPERT_ROWS - 1) + EXPERT_ROWS - 1) // EXPERT_ROWS) * EXPERT_ROWS
    n_blocks = n_slots // EXPERT_ROWS
    cnt = counts[0, :N_EXPERTS].astype(jnp.int32)
    padded = ((cnt + EXPERT_ROWS - 1) // EXPERT_ROWS) * EXPERT_ROWS
    padded_end = jnp.cumsum(padded)
    start_pad = padded_end - padded
    expert = route[:, :TOP_K]
    rank = route[:, TOP_K:2 * TOP_K]
    dest = (start_pad[expert] + rank).reshape(nk)
    n_pad = n_slots - nk
    pad_cnt = jnp.concatenate([padded - cnt, n_slots - padded_end[-1:]])
    pad_end = jnp.cumsum(pad_cnt)
    pad_first_slot = jnp.concatenate([start_pad + cnt, padded_end[-1:]])
    pad_idx = jnp.arange(n_pad, dtype=jnp.int32)
    pad_owner = jnp.sum((pad_end[None, :] <= pad_idx[:, None]).astype(jnp.int32), axis=1)
    pad_slot = pad_first_slot[pad_owner] + pad_idx - (pad_end - pad_cnt)[pad_owner]
    dst_slot = jnp.concatenate([dest, pad_slot])
    block_start = jnp.arange(n_blocks, dtype=jnp.int32) * EXPERT_ROWS
    block_expert = jnp.minimum(
        jnp.sum((padded_end[None, :] <= block_start[:, None]).astype(jnp.int32), axis=1), N_EXPERTS - 1)
    n_used = (padded_end[-1:] // EXPERT_ROWS).astype(jnp.int32)

    xs = _dispatch(dst_slot, hn, per_step=EXPERT_ROWS)
    ys = _experts(block_expert, n_used, xs, w_gate_up, b_gate_up, w_down, b_down)
    out = _combine(dest, ys, h, prob, norm_final_g.reshape(1, d), tm=tm_out)
    return out.reshape(batch, seq, d)
```

```python
import functools

import jax
import jax.numpy as jnp
from jax import lax
from jax.experimental import pallas as pl
from jax.experimental.pallas import tpu as pltpu

D_MODEL = 1024
N_HEADS = 16
HEAD_DIM = 64
HALF_DIM = HEAD_DIM // 2
CONV_WIDTH = 3
MOBA_BLOCK = 256
TOP_BLOCKS = 3
ROPE_THETA = 10000.0
N_EXPERTS = 32
TOP_K = 4
D_FF = D_MODEL
SWIGLU_LIMIT = 7.0
SWIGLU_ALPHA = 1.702
EXPERT_ROWS = 512
EPS = 1e-5
IN_PROJ_CHUNKS = 8

LANES = 128
SUBLANES = 8
PACKED_ROWS = 2 * SUBLANES
HEADS_PER_GROUP = LANES // HEAD_DIM
N_HEAD_GROUPS = N_HEADS // HEADS_PER_GROUP
MOBA_GROUP = 4
MOBA_QBLOCKS = 4
LOG2_E = 1.4426950408889634
MAX_LAGGED_RISE = 60.0
MASKED = -1e30
VMEM_LIMIT = 56 * 1024 * 1024

_MXU_DTYPE = jnp.bfloat16
F32 = jnp.float32


def _dot(a, b):
    return jnp.dot(a, b, preferred_element_type=F32)


def _sigmoid(t):
    return 1.0 / (1.0 + jnp.exp(-t))


def _rope_cols(z, cos, sin_signed):
    lane = lax.broadcasted_iota(jnp.int32, (1, LANES), 1)
    first_half = (lane % HEAD_DIM) < HALF_DIM
    cols = []
    for g in range(D_MODEL // LANES):
        t = z[:, g * LANES:(g + 1) * LANES]
        from_above = pltpu.roll(t, LANES - HALF_DIM, axis=1)
        from_below = pltpu.roll(t, HALF_DIM, axis=1)
        cols.append(t * cos + jnp.where(first_half, from_above, from_below) * sin_signed)
    return jnp.concatenate(cols, axis=1)


def _inproj_kernel(x_ref, g_ref, w_ref, cos_ref, sin_ref,
                   zc_ref, qt_ref, k_ref, vt_ref, gates_ref, kmean_ref, xn_ref, *, tm):
    j = pl.program_id(1)
    kb_per_tile = tm // MOBA_BLOCK

    @pl.when(j == 0)
    def _():
        x = x_ref[...]
        ms = jnp.mean(x * x, axis=-1, keepdims=True)
        xn_ref[...] = ((x * lax.rsqrt(ms + EPS)) * g_ref[...]).astype(xn_ref.dtype)

    z = _dot(xn_ref[...], w_ref[...])

    @pl.when(j < 3)
    def _():
        zc_ref[...] = z

    @pl.when(j == 3)
    def _():
        qt = (_rope_cols(z, cos_ref[...], sin_ref[...]) * (LOG2_E * HEAD_DIM ** -0.5)).T
        for c in range(kb_per_tile):
            qt_ref[c] = qt[:, c * MOBA_BLOCK:(c + 1) * MOBA_BLOCK].astype(qt_ref.dtype)

    @pl.when(j == 4)
    def _():
        kr = _rope_cols(z, cos_ref[...], sin_ref[...])
        for c in range(kb_per_tile):
            blk = kr[c * MOBA_BLOCK:(c + 1) * MOBA_BLOCK, :]
            k_ref[c] = blk.astype(k_ref.dtype)
            kmean_ref[0, c:c + 1, :] = jnp.mean(blk, axis=0, keepdims=True)

    @pl.when(j == 5)
    def _():
        vt = z.T
        for c in range(kb_per_tile):
            vt_ref[c] = vt[:, c * MOBA_BLOCK:(c + 1) * MOBA_BLOCK].astype(vt_ref.dtype)

    @pl.when(j >= 6)
    def _():
        gates_ref[...] = z


def _inproj(x2, g, w_in, cos, sin_signed, *, seq, tm):
    n = x2.shape[0]
    n_tiles = n // tm
    tiles_per_seq = seq // tm
    kb_per_tile = tm // MOBA_BLOCK
    n_kb = n // MOBA_BLOCK
    act = _MXU_DTYPE
    out_shape = (
        jax.ShapeDtypeStruct((n, 3 * D_MODEL), F32),
        jax.ShapeDtypeStruct((n_kb, D_MODEL, MOBA_BLOCK), act),
        jax.ShapeDtypeStruct((n_kb, MOBA_BLOCK, D_MODEL), act),
        jax.ShapeDtypeStruct((n_kb, D_MODEL, MOBA_BLOCK), act),
        jax.ShapeDtypeStruct((n, 2 * D_MODEL), F32),
        jax.ShapeDtypeStruct((n_tiles, kb_per_tile, D_MODEL), F32),
    )
    row_tile = lambda i, j: (i, 0)
    blk3 = lambda i, j: (i, 0, 0)
    return pl.pallas_call(
        functools.partial(_inproj_kernel, tm=tm),
        out_shape=out_shape,
        grid=(n_tiles, IN_PROJ_CHUNKS),
        in_specs=[
            pl.BlockSpec((tm, D_MODEL), row_tile),
            pl.BlockSpec((1, D_MODEL), lambda i, j: (0, 0)),
            pl.BlockSpec((D_MODEL, D_MODEL), lambda i, j: (0, j)),
            pl.BlockSpec((tm, LANES), lambda i, j: (i % tiles_per_seq, 0)),
            pl.BlockSpec((tm, LANES), lambda i, j: (i % tiles_per_seq, 0)),
        ],
        out_specs=(
            pl.BlockSpec((tm, D_MODEL), lambda i, j: (i, jnp.minimum(j, 2))),
            pl.BlockSpec((kb_per_tile, D_MODEL, MOBA_BLOCK), blk3),
            pl.BlockSpec((kb_per_tile, MOBA_BLOCK, D_MODEL), blk3),
            pl.BlockSpec((kb_per_tile, D_MODEL, MOBA_BLOCK), blk3),
            pl.BlockSpec((tm, D_MODEL), lambda i, j: (i, jnp.clip(j - 6, 0, 1))),
            pl.BlockSpec((1, kb_per_tile, D_MODEL), blk3),
        ),
        scratch_shapes=[pltpu.VMEM((tm, D_MODEL), act)],
        compiler_params=pltpu.CompilerParams(
            dimension_semantics=("arbitrary", "arbitrary"), vmem_limit_bytes=VMEM_LIMIT),
        name="inproj",
    )(x2, g, w_in, cos, sin_signed)


def _moba_kernel(qt_ref, k_ref, vt_ref, kmean_ref, onehot_ref, o_ref, qaug_ref, p_ref,
                 *, n_kb, group, qpb):
    qb0 = pl.program_id(2) * qpb
    act = qt_ref.dtype
    blk = MOBA_BLOCK
    width = qpb * blk
    heads = range(HEADS_PER_GROUP)
    lanes_of = lambda c: slice(c * blk, (c + 1) * blk)
    qt = jnp.concatenate([qt_ref[c] for c in range(qpb)], axis=1)
    km = kmean_ref[0].astype(act)
    dim_row = lax.broadcasted_iota(jnp.int32, (LANES, 1), 0)
    kb_row = lax.broadcasted_iota(jnp.int32, (n_kb, 1), 0)
    lane_qb = qb0 + lax.broadcasted_iota(jnp.int32, (1, width), 1) // blk
    n_bias_rows = LANES

    for h in heads:
        in_head = (dim_row >= h * HEAD_DIM) & (dim_row < (h + 1) * HEAD_DIM)
        qh = jnp.where(in_head, qt, jnp.zeros_like(qt))
        gate = _dot(km, qh)
        work = jnp.where(kb_row < lane_qb, gate, -jnp.inf)
        sel = jnp.zeros(gate.shape, jnp.bool_)
        for _ in range(TOP_BLOCKS):
            top = jnp.max(work, axis=0, keepdims=True)
            first = jnp.min(jnp.where(work == top, kb_row, n_kb), axis=0, keepdims=True)
            hit = kb_row == first
            sel = sel | (hit & (top > -jnp.inf))
            work = jnp.where(hit, -jnp.inf, work)
        bias = jnp.where(sel, 0.0, MASKED).astype(act)
        pad = jnp.zeros((n_bias_rows - n_kb, width), act)
        qaug_ref[h] = jnp.concatenate([qh, bias, pad], axis=0)

    def v_rows(kb, h):
        return vt_ref[kb, h * HEAD_DIM:(h + 1) * HEAD_DIM, :]

    key_idx = lax.broadcasted_iota(jnp.int32, (blk, blk), 0)
    qry_idx = lax.broadcasted_iota(jnp.int32, (blk, blk), 1)
    causal = key_idx <= qry_idx
    ones_own = jnp.ones((PACKED_ROWS, blk), act)

    def own_scores(h, c):
        k_own = jnp.concatenate([k_ref[qb0 + c], jnp.zeros((blk, LANES), act)], axis=1)
        return jnp.where(causal, _dot(k_own, qaug_ref[h, :, lanes_of(c)]), MASKED)

    def own_pv(h, c, p):
        return _dot(jnp.concatenate([v_rows(qb0 + c, h), ones_own], axis=0), p.astype(act))

    rows = group * blk
    n_groups = jnp.maximum((qb0 + qpb - 1 + group - 1) // group, 1)
    ones_rows = jnp.ones((PACKED_ROWS, rows), act)

    def k_operand(j):
        kb0 = pl.multiple_of(j * group, group)
        k_big = k_ref[pl.ds(kb0, group)].reshape(rows, LANES)
        pick = onehot_ref[pl.ds(kb0, group)].reshape(rows, LANES)
        return jnp.concatenate([k_big, pick], axis=1)

    def v_operand(j, h):
        kb0 = pl.multiple_of(j * group, group)
        v_big = jnp.concatenate([v_rows(kb0 + u, h) for u in range(group)], axis=1)
        return jnp.concatenate([v_big, ones_rows], axis=0)

    def write_out(accs):
        outs = [acc[:HEAD_DIM] / acc[HEAD_DIM:HEAD_DIM + 1] for acc in accs]
        o_ref[...] = jnp.concatenate(outs, axis=0).T.astype(o_ref.dtype)

    def score_stage(j, refs, slot):
        s = [_dot(k_operand(j), qaug_ref[h]) for h in heads]
        tops = []
        for h in heads:
            p_ref[slot, h] = jnp.exp2(s[h] - refs[h]).astype(act)
            tops.append(jnp.max(s[h], axis=0, keepdims=True))
        return tops

    def pv_stage(j, slot):
        return [_dot(v_operand(j, h), p_ref[slot, h]) for h in heads]

    def pipelined_body(j, carry):
        pv = pv_stage(j - 1, (j - 1) % 2)
        tops = score_stage(j, [carry[5 * h + 1] for h in heads], j % 2)
        new = []
        for h in heads:
            ref_prev, ref, acc, rise, peak = carry[5 * h:5 * h + 5]
            new += [ref, jnp.maximum(ref, tops[h]), (acc + pv[h]) * jnp.exp2(ref_prev - ref),
                    jnp.maximum(rise, tops[h] - ref), jnp.maximum(peak, tops[h])]
        return tuple(new)

    zero = jnp.zeros((1, width), F32)
    tops = score_stage(0, [zero for _ in heads], 0)
    init = []
    for h in heads:
        init += [zero, jnp.maximum(zero, tops[h]), jnp.zeros((HEAD_DIM + PACKED_ROWS, width), F32),
                 tops[h], tops[h]]
    fast = lax.fori_loop(1, n_groups, pipelined_body, tuple(init))
    pv = pv_stage(n_groups - 1, (n_groups - 1) % 2)
    accs, unsafe = [], []
    for h in heads:
        ref_prev, ref, acc, rise, peak = fast[5 * h:5 * h + 5]
        past = (acc + pv[h]) * jnp.exp2(ref_prev - ref)
        pieces = []
        for c in range(qpb):
            s = own_scores(h, c)
            top = jnp.max(s, axis=0, keepdims=True)
            ref_c = ref[:, lanes_of(c)]
            pieces.append(past[:, lanes_of(c)] + own_pv(h, c, jnp.exp2(s - ref_c)))
            unsafe.append(jnp.maximum(jnp.maximum(rise[:, lanes_of(c)], top - ref_c),
                                      -jnp.maximum(peak[:, lanes_of(c)], top)))
        accs.append(jnp.concatenate(pieces, axis=1))
    write_out(accs)
    worst = jnp.max(functools.reduce(jnp.maximum, unsafe))

    @pl.when(worst > MAX_LAGGED_RISE)
    def _():
        own = []
        for h in heads:
            ms, pvs = [], []
            for c in range(qpb):
                s = own_scores(h, c)
                m = jnp.max(s, axis=0, keepdims=True)
                ms.append(m)
                pvs.append(own_pv(h, c, jnp.exp2(s - m)))
            own += [jnp.concatenate(ms, axis=1), jnp.concatenate(pvs, axis=1)]

        def exact_body(j, carry):
            k_aug = k_operand(j)
            new = []
            for h in heads:
                m, acc = carry[2 * h], carry[2 * h + 1]
                m_new = jnp.maximum(m, jnp.max(_dot(k_aug, qaug_ref[h]), axis=0, keepdims=True))
                p_ref[0, h] = jnp.exp2(_dot(k_aug, qaug_ref[h]) - m_new).astype(act)
                new += [m_new, jnp.exp2(m - m_new) * acc + _dot(v_operand(j, h), p_ref[0, h])]
            return tuple(new)

        safe = lax.fori_loop(0, n_groups, exact_body, tuple(own))
        write_out([safe[2 * h + 1] for h in heads])


def _moba(qt, k, vt, kmean, *, batch, n_kb):
    n = batch * n_kb * MOBA_BLOCK
    group = MOBA_GROUP if n_kb % MOBA_GROUP == 0 else 1
    qpb = MOBA_QBLOCKS if n_kb % MOBA_QBLOCKS == 0 else 1
    width = qpb * MOBA_BLOCK
    steps = n_kb // qpb
    onehot = jnp.arange(LANES)[None, None, :] == jnp.arange(n_kb)[:, None, None]
    onehot = jnp.broadcast_to(onehot, (n_kb, MOBA_BLOCK, LANES)).astype(qt.dtype)
    return pl.pallas_call(
        functools.partial(_moba_kernel, n_kb=n_kb, group=group, qpb=qpb),
        out_shape=jax.ShapeDtypeStruct((n, D_MODEL), qt.dtype),
        grid=(batch, N_HEAD_GROUPS, steps),
        in_specs=[
            pl.BlockSpec((qpb, LANES, MOBA_BLOCK), lambda b, g, q: (b * steps + q, g, 0)),
            pl.BlockSpec((n_kb, MOBA_BLOCK, LANES), lambda b, g, q: (b, 0, g)),
            pl.BlockSpec((n_kb, LANES, MOBA_BLOCK), lambda b, g, q: (b, g, 0)),
            pl.BlockSpec((1, n_kb, LANES), lambda b, g, q: (b, 0, g)),
            pl.BlockSpec((n_kb, MOBA_BLOCK, LANES), lambda b, g, q: (0, 0, 0)),
        ],
        out_specs=pl.BlockSpec((width, LANES), lambda b, g, q: (b * steps + q, g)),
        scratch_shapes=[
            pltpu.VMEM((HEADS_PER_GROUP, 2 * LANES, width), qt.dtype),
            pltpu.VMEM((2, HEADS_PER_GROUP, group * MOBA_BLOCK, width), qt.dtype),
        ],
        compiler_params=pltpu.CompilerParams(
            dimension_semantics=("arbitrary", "arbitrary", "arbitrary"), vmem_limit_bytes=VMEM_LIMIT),
        name="moba",
    )(qt, k, vt, kmean, onehot)


def _mix_kernel(cb_ref, cc_ref, cx_ref, cch_ref, cxh_ref, gc_ref, ga_ref, attn_ref, x_ref,
                convw_ref, wco_ref, wao_ref, wo_ref, g_ref, rw_ref, rb_ref, tri_ref,
                h_ref, hn_ref, route_ref, prob_ref, counts_ref, carry_ref, *, tm, tiles_per_seq):
    i = pl.program_id(0)
    act = wco_ref.dtype

    @pl.when(i == 0)
    def _():
        carry_ref[...] = jnp.zeros_like(carry_ref)

    u = cc_ref[...] * cx_ref[...]
    keep = (i % tiles_per_seq != 0).astype(F32)
    halo = cch_ref[...] * cxh_ref[...] * keep
    prev1, prev2 = halo[SUBLANES - 1:SUBLANES, :], halo[SUBLANES - 2:SUBLANES - 1, :]
    row = lax.broadcasted_iota(jnp.int32, (tm, 1), 0)
    u1 = jnp.where(row == 0, prev1, pltpu.roll(u, 1, axis=0))
    u2 = jnp.where(row == 0, prev2, jnp.where(row == 1, prev1, pltpu.roll(u, 2, axis=0)))
    cw = convw_ref[...]
    conv = cw[0:1, :] * u2 + cw[1:2, :] * u1 + cw[2:3, :] * u
    y_conv = _dot((cb_ref[...] * conv).astype(act), wco_ref[...])
    y_attn = _dot(attn_ref[...], wao_ref[...])
    merged = _sigmoid(gc_ref[...]) * y_conv + _sigmoid(ga_ref[...]) * y_attn
    h = x_ref[...] + _dot(merged.astype(act), wo_ref[...])
    h_ref[...] = h
    hn = (h * lax.rsqrt(jnp.mean(h * h, axis=-1, keepdims=True) + EPS)) * g_ref[...]
    hn_ref[...] = hn

    logits = _dot(hn.astype(act), rw_ref[...]) + rb_ref[...]
    lane = lax.broadcasted_iota(jnp.int32, (1, LANES), 1)
    work = logits
    picks, vals = [], []
    sel = jnp.zeros(logits.shape, jnp.bool_)
    for _ in range(TOP_K):
        top = jnp.max(work, axis=-1, keepdims=True)
        first = jnp.min(jnp.where(work == top, lane, LANES), axis=-1, keepdims=True)
        hit = lane == first
        sel = sel | hit
        work = jnp.where(hit, -jnp.inf, work)
        picks.append(first)
        vals.append(top)
    exps = [jnp.exp(v - vals[0]) for v in vals]
    denom = exps[0] + exps[1] + exps[2] + exps[3]

    sel_f = jnp.where(sel, 1.0, 0.0)
    before = _dot(tri_ref[...], sel_f.astype(act)) + carry_ref[0:1, :]
    carry_ref[...] = carry_ref[...] + jnp.sum(sel_f, axis=0, keepdims=True)
    counts_ref[...] = carry_ref[...]

    route = jnp.zeros(logits.shape, jnp.int32)
    prob = jnp.zeros(logits.shape, F32)
    for t in range(TOP_K):
        rank = jnp.sum(jnp.where(lane == picks[t], before, 0.0), axis=-1, keepdims=True)
        route = jnp.where(lane == t, picks[t], route)
        route = jnp.where(lane == TOP_K + t, rank.astype(jnp.int32), route)
        prob = jnp.where(lane == t, exps[t] / denom, prob)
    route_ref[...] = route
    prob_ref[...] = prob


def _mix(zc, gates, attn, x2, conv_w, w_conv_out, w_attn_out, w_out, g, router_w, router_b, tri,
         *, seq, tm):
    n = x2.shape[0]
    n_tiles = n // tm
    halo_blocks = tm // SUBLANES
    row_tile = lambda i: (i, 0)
    const = lambda i: (0, 0)
    col = lambda c: (lambda i: (i, c))
    halo = lambda c: (lambda i: (jnp.maximum(i * halo_blocks - 1, 0), c))
    full = lambda a: pl.BlockSpec(a.shape, const)
    out_shape = (
        jax.ShapeDtypeStruct((n, D_MODEL), F32),
        jax.ShapeDtypeStruct((n, D_MODEL), F32),
        jax.ShapeDtypeStruct((n, LANES), jnp.int32),
        jax.ShapeDtypeStruct((n, LANES), F32),
        jax.ShapeDtypeStruct((SUBLANES, LANES), F32),
    )
    return pl.pallas_call(
        functools.partial(_mix_kernel, tm=tm, tiles_per_seq=seq // tm),
        out_shape=out_shape,
        grid=(n_tiles,),
        in_specs=[
            pl.BlockSpec((tm, D_MODEL), col(0)), pl.BlockSpec((tm, D_MODEL), col(1)),
            pl.BlockSpec((tm, D_MODEL), col(2)),
            pl.BlockSpec((SUBLANES, D_MODEL), halo(1)), pl.BlockSpec((SUBLANES, D_MODEL), halo(2)),
            pl.BlockSpec((tm, D_MODEL), col(0)), pl.BlockSpec((tm, D_MODEL), col(1)),
            pl.BlockSpec((tm, D_MODEL), row_tile), pl.BlockSpec((tm, D_MODEL), row_tile),
            full(conv_w), full(w_conv_out), full(w_attn_out), full(w_out), full(g),
            full(router_w), full(router_b), full(tri),
        ],
        out_specs=(
            pl.BlockSpec((tm, D_MODEL), row_tile), pl.BlockSpec((tm, D_MODEL), row_tile),
            pl.BlockSpec((tm, LANES), row_tile), pl.BlockSpec((tm, LANES), row_tile),
            pl.BlockSpec((SUBLANES, LANES), const),
        ),
        scratch_shapes=[pltpu.VMEM((SUBLANES, LANES), F32)],
        compiler_params=pltpu.CompilerParams(
            dimension_semantics=("arbitrary",), vmem_limit_bytes=VMEM_LIMIT),
        name="mix",
    )(zc, zc, zc, zc, zc, gates, gates, attn, x2, conv_w, w_conv_out, w_attn_out, w_out, g,
      router_w, router_b, tri)


def _row_copy(src, src_row, dst, dst_row, sem):
    return pltpu.make_async_copy(src.at[pl.ds(src_row, 1), :], dst.at[pl.ds(dst_row, 1), :], sem)


def _start_rows(copy_of_row, n_rows):
    for r in range(n_rows):
        copy_of_row(r).start()


def _wait_rows(copy_of_row, n_rows):
    def wait(r, c):
        copy_of_row(r).wait()
        return c
    lax.fori_loop(0, n_rows, wait, 0, unroll=8)


def _dispatch_kernel(dst_ref, hn_ref, xs_hbm, stage, sem, *, per_step):
    i = pl.program_id(0)
    last = pl.num_programs(0) - 1
    slot = i % 2
    stage[slot] = hn_ref[...]
    _start_rows(lambda r: _row_copy(stage.at[slot], r // TOP_K, xs_hbm, dst_ref[0, 0, r], sem.at[slot]), per_step)

    @pl.when(i > 0)
    def _():
        _wait_rows(lambda r: _row_copy(stage.at[1 - slot], 0, xs_hbm, 0, sem.at[1 - slot]), per_step)

    @pl.when(i == last)
    def _():
        _wait_rows(lambda r: _row_copy(stage.at[slot], 0, xs_hbm, 0, sem.at[slot]), per_step)


def _dispatch(dst, hn, *, per_step):
    n_slots = dst.shape[0]
    steps = n_slots // per_step
    tokens = per_step // TOP_K
    last_block = hn.shape[0] // tokens - 1
    assert steps * per_step == n_slots and tokens * TOP_K == per_step
    return pl.pallas_call(
        functools.partial(_dispatch_kernel, per_step=per_step),
        out_shape=jax.ShapeDtypeStruct((n_slots, D_MODEL), hn.dtype),
        grid=(steps,),
        in_specs=[
            pl.BlockSpec((1, 1, per_step), lambda i: (i, 0, 0), memory_space=pltpu.SMEM),
            pl.BlockSpec((tokens, D_MODEL), lambda i: (jnp.minimum(i, last_block), 0)),
        ],
        out_specs=pl.BlockSpec(memory_space=pl.ANY),
        scratch_shapes=[pltpu.VMEM((2, tokens, D_MODEL), hn.dtype), pltpu.SemaphoreType.DMA((2,))],
        compiler_params=pltpu.CompilerParams(dimension_semantics=("arbitrary",)),
        name="dispatch",
    )(dst.reshape(steps, 1, per_step), hn)


def _experts_kernel(be_ref, nused_ref, xs_ref, wgu_ref, bgu_ref, wd_ref, bd_ref, ys_ref, wgu_act, wd_act):
    i = pl.program_id(0)
    act = wgu_act.dtype

    @pl.when((i == 0) | (be_ref[i] != be_ref[jnp.maximum(i - 1, 0)]))
    def _():
        wgu_act[...] = wgu_ref[0].astype(act)
        wd_act[...] = wd_ref[0].astype(act)

    @pl.when(i < nused_ref[0])
    def _():
        gu = _dot(xs_ref[...].astype(act), wgu_act[...]) + bgu_ref[0]
        gate = jnp.minimum(gu[:, :D_FF], SWIGLU_LIMIT)
        up = jnp.clip(gu[:, D_FF:], -SWIGLU_LIMIT, SWIGLU_LIMIT)
        a = (up + 1.0) * (gate * _sigmoid(SWIGLU_ALPHA * gate))
        ys_ref[...] = _dot(a.astype(act), wd_act[...]) + bd_ref[0]

    @pl.when(i >= nused_ref[0])
    def _():
        ys_ref[...] = jnp.zeros_like(ys_ref)


def _experts(block_expert, n_used, xs, w_gate_up, b_gate_up, w_down, b_down):
    n_blocks = block_expert.shape[0]
    row_blk = pl.BlockSpec((EXPERT_ROWS, D_MODEL), lambda i, be, nu: (i, 0))
    grid_spec = pltpu.PrefetchScalarGridSpec(
        num_scalar_prefetch=2,
        grid=(n_blocks,),
        in_specs=[
            row_blk,
            pl.BlockSpec((1, D_MODEL, 2 * D_FF), lambda i, be, nu: (be[i], 0, 0)),
            pl.BlockSpec((1, 1, 2 * D_FF), lambda i, be, nu: (be[i], 0, 0)),
            pl.BlockSpec((1, D_FF, D_MODEL), lambda i, be, nu: (be[i], 0, 0)),
            pl.BlockSpec((1, 1, D_MODEL), lambda i, be, nu: (be[i], 0, 0)),
        ],
        out_specs=row_blk,
        scratch_shapes=[
            pltpu.VMEM((D_MODEL, 2 * D_FF), _MXU_DTYPE),
            pltpu.VMEM((D_FF, D_MODEL), _MXU_DTYPE),
        ],
    )
    return pl.pallas_call(
        _experts_kernel,
        out_shape=jax.ShapeDtypeStruct((n_blocks * EXPERT_ROWS, D_MODEL), F32),
        grid_spec=grid_spec,
        compiler_params=pltpu.CompilerParams(
            dimension_semantics=("arbitrary",), vmem_limit_bytes=VMEM_LIMIT),
        name="experts",
    )(block_expert, n_used, xs, w_gate_up,
      b_gate_up.reshape(N_EXPERTS, 1, 2 * D_FF), w_down, b_down.reshape(N_EXPERTS, 1, D_MODEL))


def _combine_kernel(dest_ref, dest_next_ref, ys_hbm, h_ref, prob_ref, g_ref, o_ref, ybuf, sem, *, tm):
    i = pl.program_id(0)
    last = pl.num_programs(0) - 1
    slot = i % 2

    def start_gather(dests, s):
        for k in range(TOP_K):
            _start_rows(lambda r: _row_copy(ys_hbm, dests[0, 0, r * TOP_K + k], ybuf.at[s, k], r, sem.at[s]), tm)

    @pl.when(i == 0)
    def _():
        start_gather(dest_ref, 0)

    @pl.when(i < last)
    def _():
        start_gather(dest_next_ref, 1 - slot)

    for k in range(TOP_K):
        _wait_rows(lambda r: _row_copy(ys_hbm, 0, ybuf.at[slot, k], r, sem.at[slot]), tm)
    prob = prob_ref[...]
    h = h_ref[...]
    for k in range(TOP_K):
        h = h + prob[:, k:k + 1] * ybuf[slot, k]
    o_ref[...] = (h * lax.rsqrt(jnp.mean(h * h, axis=-1, keepdims=True) + EPS)) * g_ref[...]


def _combine(dest, ys, h, prob, g, *, tm):
    n = h.shape[0]
    n_tiles = n // tm
    dest3 = dest.reshape(n_tiles, 1, tm * TOP_K)
    smem_blk = lambda f: pl.BlockSpec((1, 1, tm * TOP_K), f, memory_space=pltpu.SMEM)
    row_tile = lambda i: (i, 0)
    return pl.pallas_call(
        functools.partial(_combine_kernel, tm=tm),
        out_shape=jax.ShapeDtypeStruct((n, D_MODEL), F32),
        grid=(n_tiles,),
        in_specs=[
            smem_blk(lambda i: (i, 0, 0)),
            smem_blk(lambda i: (jnp.minimum(i + 1, n_tiles - 1), 0, 0)),
            pl.BlockSpec(memory_space=pl.ANY),
            pl.BlockSpec((tm, D_MODEL), row_tile),
            pl.BlockSpec((tm, LANES), row_tile),
            pl.BlockSpec((1, D_MODEL), lambda i: (0, 0)),
        ],
        out_specs=pl.BlockSpec((tm, D_MODEL), row_tile),
        scratch_shapes=[
            pltpu.VMEM((2, TOP_K, tm, D_MODEL), F32),
            pltpu.SemaphoreType.DMA((2,)),
        ],
        compiler_params=pltpu.CompilerParams(
            dimension_semantics=("arbitrary",), vmem_limit_bytes=VMEM_LIMIT),
        name="combine",
    )(dest3, dest3, ys, h, prob, g)


def _rope_tables(seq):
    inv_freq = ROPE_THETA ** (-jnp.arange(0, HALF_DIM, dtype=F32) / HALF_DIM)
    ang = jnp.arange(seq, dtype=jnp.int32).astype(F32)[:, None] * inv_freq[None, :]
    cos, sin = jnp.cos(ang), jnp.sin(ang)
    reps = LANES // HEAD_DIM
    cos_t = jnp.tile(jnp.concatenate([cos, cos], axis=1), (1, reps))
    sin_t = jnp.tile(jnp.concatenate([-sin, sin], axis=1), (1, reps))
    return cos_t, sin_t


def _tile_rows(seq, want):
    tm = min(want, seq)
    assert seq % tm == 0 and tm % MOBA_BLOCK == 0, (seq, tm)
    return tm


def kernel(x, norm_mix_g, w_in, conv_w, w_conv_out, w_attn_out, w_out, norm_ffn_g, router_w, router_b,
           w_gate_up, b_gate_up, w_down, b_down, norm_final_g):
    batch, seq, d = x.shape
    assert d == D_MODEL and seq % MOBA_BLOCK == 0
    n = batch * seq
    n_kb = seq // MOBA_BLOCK
    act = _MXU_DTYPE
    x2 = x.reshape(n, d)
    tm_proj = _tile_rows(seq, 512)
    tm_mix = _tile_rows(seq, 256)
    tm_out = _tile_rows(seq, 256)

    cos_t, sin_t = _rope_tables(seq)
    zc, qt, k, vt, gates, kmean = _inproj(
        x2, norm_mix_g.reshape(1, d), w_in.astype(act), cos_t, sin_t, seq=seq, tm=tm_proj)
    attn = _moba(qt, k, vt, kmean.reshape(batch, n_kb, d), batch=batch, n_kb=n_kb)

    router_w_pad = jnp.zeros((d, LANES), F32).at[:, :N_EXPERTS].set(router_w).astype(act)
    router_b_pad = jnp.full((1, LANES), MASKED, F32).at[0, :N_EXPERTS].set(router_b)
    tri = (jnp.arange(tm_mix)[:, None] > jnp.arange(tm_mix)[None, :]).astype(act)
    h, hn, route, prob, counts = _mix(
        zc, gates, attn, x2, conv_w, w_conv_out.astype(act), w_attn_out.astype(act), w_out.astype(act),
        norm_ffn_g.reshape(1, d), router_w_pad, router_b_pad, tri, seq=seq, tm=tm_mix)

    nk = n * TOP_K
    n_slots = ((nk + N_EXPERTS * (EXPERT_ROWS - 1) + EXPERT_ROWS - 1) // EXPERT_ROWS) * EXPERT_ROWS
    n_blocks = n_slots // EXPERT_ROWS
    cnt = counts[0, :N_EXPERTS].astype(jnp.int32)
    padded = ((cnt + EXPERT_ROWS - 1) // EXPERT_ROWS) * EXPERT_ROWS
    padded_end = jnp.cumsum(padded)
    start_pad = padded_end - padded
    expert = route[:, :TOP_K]
    rank = route[:, TOP_K:2 * TOP_K]
    dest = (start_pad[expert] + rank).reshape(nk)
    n_pad = n_slots - nk
    pad_cnt = jnp.concatenate([padded - cnt, n_slots - padded_end[-1:]])
    pad_end = jnp.cumsum(pad_cnt)
    pad_first_slot = jnp.concatenate([start_pad + cnt, padded_end[-1:]])
    pad_idx = jnp.arange(n_pad, dtype=jnp.int32)
    pad_owner = jnp.sum((pad_end[None, :] <= pad_idx[:, None]).astype(jnp.int32), axis=1)
    pad_slot = pad_first_slot[pad_owner] + pad_idx - (pad_end - pad_cnt)[pad_owner]
    dst_slot = jnp.concatenate([dest, pad_slot])
    block_start = jnp.arange(n_blocks, dtype=jnp.int32) * EXPERT_ROWS
    block_expert = jnp.minimum(
        jnp.sum((padded_end[None, :] <= block_start[:, None]).astype(jnp.int32), axis=1), N_EXPERTS - 1)
    n_used = (padded_end[-1:] // EXPERT_ROWS).astype(jnp.int32)

    xs = _dispatch(dst_slot, hn, per_step=EXPERT_ROWS)
    ys = _experts(block_expert, n_used, xs, w_gate_up, b_gate_up, w_down, b_down)
    out = _combine(dest, ys, h, prob, norm_final_g.reshape(1, d), tm=tm_out)
    return out.reshape(batch, seq, d)
```
